```python
import jax, jax.numpy as jnp
from jax import lax
import numpy as np

D_MODEL = 2048
BATCH = 4
SEQ = 8192
DEPTH = 1
DEC_BATCH = 16
DEC_SEQ = 32
PAST_LEN = 2048

CHUNK = 64
CONV_WIDTH = 3
D_CONV = D_MODEL // 2
D_GLA = D_MODEL - D_CONV
GLA_HEADS = 4
GLA_DK = D_GLA // 2
HEAD_K = GLA_DK // GLA_HEADS
HEAD_V = D_GLA // GLA_HEADS
GATE_RANK = 16
GATE_TAU = 16.0
D_FF = -(-8 * D_MODEL // (3 * 256)) * 256
SPLIT_SIZES = (D_CONV, D_CONV, D_CONV, GLA_DK, GLA_DK, D_GLA, D_GLA, GATE_RANK)
D_IN = sum(SPLIT_SIZES)
SPLIT_POINTS = tuple(int(s) for s in np.cumsum(SPLIT_SIZES)[:-1])
ALPHA = (2.0 * DEPTH) ** 0.25
BETA = (8.0 * DEPTH) ** -0.25
LN_EPS = 1e-5
RMS_EPS = 1e-6

kernel_name = 'hybrid_conv_gla_stream'


def _layernorm(x, g, b):
    xf = x.astype(jnp.float32)
    mu = jnp.mean(xf, axis=-1, keepdims=True)
    var = jnp.mean(jnp.square(xf - mu), axis=-1, keepdims=True)
    return ((xf - mu) * lax.rsqrt(var + LN_EPS) * g + b).astype(x.dtype)


def _to_blocks(a, n, L):
    Bn, _, H, d = a.shape
    return a.reshape(Bn, n, L, H, d).transpose(1, 0, 3, 2, 4)


def _gla(q, k, v, logf, s0, L):
    Bn, T, H, _ = q.shape
    n = T // L
    mask = jnp.tril(jnp.ones((L, L), dtype=bool))[None, None, :, :, None]

    def step(S, inp):
        qc, kc, vc, fc = inp
        b = jnp.cumsum(fc, axis=2)
        diff = b[:, :, :, None, :] - b[:, :, None, :, :]
        decay = jnp.exp(jnp.where(mask, diff, -jnp.inf))
        A = jnp.einsum('bhid,bhjd,bhijd->bhij', qc, kc, decay)
        o = (jnp.einsum('bhij,bhjv->bhiv', A, vc)
             + jnp.einsum('bhid,bhdv->bhiv', qc * jnp.exp(b), S))
        bL = b[:, :, -1:, :]
        S = (jnp.exp(bL[:, :, 0, :])[..., None] * S
             + jnp.einsum('bhjd,bhjv->bhdv', kc * jnp.exp(bL - b), vc))
        return S, o

    f32 = jnp.float32
    xs = (_to_blocks(q.astype(f32), n, L), _to_blocks(k.astype(f32), n, L),
          _to_blocks(v.astype(f32), n, L), _to_blocks(logf, n, L))
    S, o = lax.scan(step, s0.astype(f32), xs)
    o = o.transpose(1, 0, 3, 2, 4).reshape(Bn, T, H, v.shape[-1])
    return o, S


def _layer(x, c, conv_state, gla_state, L, w_ada, b_ada, w_in, w_conv, w_f, b_f,
           gla_gain, w_o, ln1_g, ln1_b, w_gate, w_up, w_down, ln2_g, ln2_b):
    Bn, T, _ = x.shape
    mod = jax.nn.silu(c) @ w_ada + b_ada
    sh1, sc1, g1, sh2, sc2, g2 = jnp.split(mod[:, None, :], 6, axis=-1)
    h = x * (1 + sc1) + sh1
    p = h @ w_in
    hb, hc, hin, q, k, v, go, flr = jnp.split(p, SPLIT_POINTS, axis=-1)

    u = hc * hin
    upad = jnp.concatenate([conv_state.astype(u.dtype), u], axis=1)
    conv = upad[:, 0:T] * w_conv[0]
    for i in range(1, CONV_WIDTH):
        conv = conv + upad[:, i:i + T] * w_conv[i]
    y_conv = hb * conv
    new_conv = upad[:, -(CONV_WIDTH - 1):]

    logf = jax.nn.log_sigmoid((flr @ w_f + b_f).astype(jnp.float32)) / GATE_TAU
    qh = q.reshape(Bn, T, GLA_HEADS, HEAD_K) * (HEAD_K ** -0.5)
    kh = k.reshape(Bn, T, GLA_HEADS, HEAD_K)
    vh = v.reshape(Bn, T, GLA_HEADS, HEAD_V)
    fh = logf.reshape(Bn, T, GLA_HEADS, HEAD_K)
    o, s_new = _gla(qh, kh, vh, fh, gla_state, L)
    o = o * lax.rsqrt(jnp.mean(jnp.square(o), axis=-1, keepdims=True) + RMS_EPS) * gla_gain
    o = o.reshape(Bn, T, D_GLA).astype(x.dtype) * jax.nn.silu(go)

    mix = jnp.concatenate([y_conv, o], axis=-1) @ w_o
    x1 = _layernorm(ALPHA * x + g1 * mix, ln1_g, ln1_b)

    h2 = x1 * (1 + sc2) + sh2
    f = (jax.nn.silu(h2 @ w_gate) * (h2 @ w_up)) @ w_down
    x2 = _layernorm(ALPHA * x1 + g2 * f, ln2_g, ln2_b)
    return x2, new_conv, s_new


def setup_inputs(seed: int = 0) -> dict:
    key = jax.random.key(seed)
    ks = iter(jax.random.split(key, 40))
    f32 = jnp.float32

    def nrm(shape, scale):
        return jax.random.normal(next(ks), shape, f32) * scale

    Ds = D_MODEL ** -0.5
    in_scales = (Ds, Ds, Ds * BETA, Ds, Ds, Ds * BETA, Ds, Ds)
    w_in = jnp.concatenate([nrm((DEPTH, D_MODEL, s), sc) for s, sc in zip(SPLIT_SIZES, in_scales)], axis=-1)
    return {
        'x_prompt': nrm((BATCH, SEQ, D_MODEL), 1.0),
        'x_sample': nrm((DEC_BATCH, DEC_SEQ, D_MODEL), 1.0),
        'c_prompt': nrm((BATCH, D_MODEL), 1.0),
        'c_sample': nrm((DEC_BATCH, D_MODEL), 1.0),
        'cache_conv': nrm((DEPTH, DEC_BATCH, CONV_WIDTH - 1, D_CONV), 0.5),
        'state_gla': nrm((DEPTH, DEC_BATCH, GLA_HEADS, HEAD_K, HEAD_V), HEAD_K ** -0.5),
        'w_ada': nrm((DEPTH, D_MODEL, 6 * D_MODEL), Ds),
        'b_ada': nrm((DEPTH, 6 * D_MODEL), 0.01),
        'w_in': w_in,
        'w_conv': nrm((DEPTH, CONV_WIDTH, D_CONV), CONV_WIDTH ** -0.5),
        'w_f': nrm((DEPTH, GATE_RANK, GLA_DK), GATE_RANK ** -0.5),
        'b_f': nrm((DEPTH, GLA_DK), 0.1),
        'gla_gain': 1.0 + nrm((DEPTH, HEAD_V), 0.01),
        'w_o': nrm((DEPTH, D_MODEL, D_MODEL), Ds * BETA),
        'ln1_g': 1.0 + nrm((DEPTH, D_MODEL), 0.01),
        'ln1_b': nrm((DEPTH, D_MODEL), 0.01),
        'w_gate': nrm((DEPTH, D_MODEL, D_FF), Ds * BETA),
        'w_up': nrm((DEPTH, D_MODEL, D_FF), Ds * BETA),
        'w_down': nrm((DEPTH, D_FF, D_MODEL), D_FF ** -0.5 * BETA),
        'ln2_g': 1.0 + nrm((DEPTH, D_MODEL), 0.01),
        'ln2_b': nrm((DEPTH, D_MODEL), 0.01),
    }


def reference(x_prompt, x_sample, c_prompt, c_sample, cache_conv, state_gla,
              w_ada, b_ada, w_in, w_conv, w_f, b_f, gla_gain, w_o,
              ln1_g, ln1_b, w_gate, w_up, w_down, ln2_g, ln2_b):
    yp, ys = x_prompt, x_sample
    bp = x_prompt.shape[0]
    conv_p, gla_p, conv_s, gla_s = [], [], [], []
    for l in range(DEPTH):
        wl = (w_ada[l], b_ada[l], w_in[l], w_conv[l], w_f[l], b_f[l], gla_gain[l], w_o[l],
              ln1_g[l], ln1_b[l], w_gate[l], w_up[l], w_down[l], ln2_g[l], ln2_b[l])
        conv0 = jnp.zeros((bp, CONV_WIDTH - 1, D_CONV), yp.dtype)
        gla0 = jnp.zeros((bp, GLA_HEADS, HEAD_K, HEAD_V), jnp.float32)
        yp, cp, sp = _layer(yp, c_prompt, conv0, gla0, CHUNK, *wl)
        ys, cs, ss = _layer(ys, c_sample, cache_conv[l], state_gla[l], ys.shape[1], *wl)
        conv_p.append(cp)
        gla_p.append(sp)
        conv_s.append(cs)
        gla_s.append(ss)
    return (yp, ys,
            jnp.stack(conv_p).astype(cache_conv.dtype),
            jnp.stack(gla_p).astype(state_gla.dtype),
            jnp.stack(conv_s).astype(cache_conv.dtype),
            jnp.stack(gla_s).astype(state_gla.dtype))
```

```python
import functools

import jax
import jax.numpy as jnp
from jax import lax
from jax.experimental import pallas as pl
from jax.experimental.pallas import tpu as pltpu

F32 = jnp.float32
BF16 = jnp.bfloat16

CONV_WIDTH = 3
GLA_HEADS = 4
GATE_TAU = 16.0
LN_EPS = 1e-5
RMS_EPS = 1e-6
PROMPT_CHUNK = 64
LANE = 128
VMEM_LIMIT = 60 * 1024 * 1024


def _layernorm(y, g, b):
    mu = jnp.mean(y, axis=-1, keepdims=True)
    d = y - mu
    var = jnp.mean(d * d, axis=-1, keepdims=True)
    return d * lax.rsqrt(var + LN_EPS) * g + b


def _silu(a):
    return a * jax.nn.sigmoid(a)


def _mod_kernel(c_ref, w_ref, b_ref, o_ref):
    a = _silu(c_ref[...]).astype(BF16)
    o_ref[...] = jnp.dot(a, w_ref[...].astype(BF16), preferred_element_type=F32) + b_ref[...]


def _modulation(c_all, w_ada, b_ada, *, tn):
    m, d = c_all.shape
    n = w_ada.shape[1]
    return pl.pallas_call(
        _mod_kernel,
        grid=(n // tn,),
        in_specs=[pl.BlockSpec((m, d), lambda j: (0, 0)),
                  pl.BlockSpec((d, tn), lambda j: (0, j)),
                  pl.BlockSpec((1, tn), lambda j: (0, j))],
        out_specs=pl.BlockSpec((m, tn), lambda j: (0, j)),
        out_shape=jax.ShapeDtypeStruct((m, n), F32),
        compiler_params=pltpu.CompilerParams(dimension_semantics=("arbitrary",),
                                             vmem_limit_bytes=VMEM_LIMIT),
        name="adaln_modulation",
    )(c_all, w_ada, b_ada.reshape(1, n))


def _mixer_kernel(x_ref, mod_ref, conv0_ref, s0_ref, w_in_ref, w_conv_ref, w_f_ref, b_f_ref,
                  gain_ref, w_o_ref, ln_g_ref, ln_b_ref,
                  x1_ref, conv_ref, s_ref,
                  h_s, hb_s, u_s, q_s, k_s, v_s, go_s, lf_s, mix_s,
                  *, nb, tc, chunk, alpha):
    d_model = x_ref.shape[-1]
    d_conv = conv_ref.shape[-1]
    head_k, head_v = s_ref.shape[-2], s_ref.shape[-1]
    gla_dk = GLA_HEADS * head_k
    d_gla = GLA_HEADS * head_v
    rows = nb * tc
    n_chunks = tc // chunk

    @pl.when(pl.program_id(1) == 0)
    def _():
        conv_ref[...] = conv0_ref[...]
        s_ref[...] = s0_ref[...]

    h = x_ref[...] * (1.0 + mod_ref[:, 1:2, :]) + mod_ref[:, 0:1, :]
    h_s[...] = h.reshape(rows, d_model).astype(BF16)

    def proj(lo, width):
        return jnp.dot(h_s[...], w_in_ref[:, lo:lo + width], preferred_element_type=F32)

    o_hb, o_hc, o_hin = 0, d_conv, 2 * d_conv
    o_q = 3 * d_conv
    o_k = o_q + gla_dk
    o_v = o_k + gla_dk
    o_go = o_v + d_gla
    o_f = o_go + d_gla
    hb_s[...] = proj(o_hb, d_conv)
    u_s[...] = proj(o_hc, d_conv) * proj(o_hin, d_conv)
    q_s[...] = proj(o_q, gla_dk) * (head_k ** -0.5)
    k_s[...] = proj(o_k, gla_dk)
    v_s[...] = proj(o_v, d_gla).astype(BF16)
    go_s[...] = proj(o_go, d_gla)
    z = jnp.dot(proj(o_f, LANE), w_f_ref[...], preferred_element_type=F32,
                precision=lax.Precision.HIGHEST) + b_f_ref[...]
    lf_s[...] = jax.nn.log_sigmoid(z) * (1.0 / GATE_TAU)

    row = lax.broadcasted_iota(jnp.int32, (tc, d_conv), 0)
    w0, w1, w2 = w_conv_ref[0:1, :], w_conv_ref[1:2, :], w_conv_ref[2:3, :]

    def conv_segment(i, carry):
        r0 = pl.multiple_of(i * tc, tc)
        u = u_s[pl.ds(r0, tc), :]
        st = conv_ref[i]
        p0, p1 = st[0:1, :], st[1:2, :]
        u1 = jnp.where(row == 0, p1, pltpu.roll(u, 1, axis=0))
        u2 = jnp.where(row == 0, p0, jnp.where(row == 1, p1, pltpu.roll(u, 2, axis=0)))
        conv = u2 * w0 + u1 * w1 + u * w2
        mix_s[pl.ds(r0, tc), 0:d_conv] = (hb_s[pl.ds(r0, tc), :] * conv).astype(BF16)
        conv_ref[i] = u[tc - (CONV_WIDTH - 1):tc, :]
        return carry

    lax.fori_loop(0, nb, conv_segment, 0)

    ri = lax.broadcasted_iota(jnp.int32, (chunk, chunk), 0)
    ci = lax.broadcasted_iota(jnp.int32, (chunk, chunk), 1)
    causal = ri >= ci
    tril = causal.astype(F32)
    ones = jnp.ones((chunk, LANE), F32)
    gain = gain_ref[...]
    contract0 = (((0,), (0,)), ((), ()))
    contract1 = (((1,), (1,)), ((), ()))

    def gla_chunk(n, carry):
        r0 = pl.multiple_of(n * chunk, chunk)
        i = n // n_chunks
        lf = lf_s[pl.ds(r0, chunk), :]
        b = jnp.dot(tril, lf, preferred_element_type=F32, precision=lax.Precision.HIGHEST)
        b_last = b[chunk - 1:chunk, :]
        q = q_s[pl.ds(r0, chunk), :]
        k = k_s[pl.ds(r0, chunk), :]
        qe = (q * jnp.exp(b)).astype(BF16)
        ke = (k * jnp.exp(-b)).astype(BF16)
        kd = (k * jnp.exp(b_last - b)).astype(BF16)
        for hd in range(GLA_HEADS):
            ks = slice(hd * head_k, (hd + 1) * head_k)
            vs = slice(hd * head_v, (hd + 1) * head_v)
            v = v_s[pl.ds(r0, chunk), vs]
            state = s_ref[i, hd]
            a = lax.dot_general(qe[:, ks], ke[:, ks], contract1, preferred_element_type=F32)
            a = jnp.where(causal, a, 0.0).astype(BF16)
            o = (jnp.dot(a, v, preferred_element_type=F32)
                 + jnp.dot(qe[:, ks], state.astype(BF16), preferred_element_type=F32))
            b_col = lax.dot_general(lf[:, ks], ones, contract0, preferred_element_type=F32,
                                    precision=lax.Precision.HIGHEST)
            decay = jnp.exp(b_col)
            decay = jnp.concatenate([decay] * (head_v // LANE), axis=1)
            s_ref[i, hd] = decay * state + lax.dot_general(kd[:, ks], v, contract0,
                                                           preferred_element_type=F32)
            o = o * lax.rsqrt(jnp.mean(o * o, axis=-1, keepdims=True) + RMS_EPS) * gain
            g = go_s[pl.ds(r0, chunk), vs]
            mix_s[pl.ds(r0, chunk), d_conv + hd * head_v:d_conv + (hd + 1) * head_v] = (
                o * _silu(g)).astype(BF16)
        return carry

    lax.fori_loop(0, nb * n_chunks, gla_chunk, 0)

    mo = jnp.dot(mix_s[...], w_o_ref[...], preferred_element_type=F32)
    y = alpha * x_ref[...] + mod_ref[:, 2:3, :] * mo.reshape(nb, tc, d_model)
    x1_ref[...] = _layernorm(y, ln_g_ref[...], ln_b_ref[...])


def _mixer(x, mod, conv0, s0, w_in, w_conv, w_f, b_f, gain, w_o, ln_g, ln_b,
           *, nb, tc, chunk, alpha):
    bsz, t, d = x.shape
    d_conv = conv0.shape[-1]
    heads, head_k, head_v = s0.shape[1:]
    gla_dk, d_gla = heads * head_k, heads * head_v
    rows = nb * tc
    const2 = lambda bi, ti: (0, 0)
    resident = functools.partial(pl.BlockSpec, index_map=const2, pipeline_mode=pl.Buffered(1))
    kern = functools.partial(_mixer_kernel, nb=nb, tc=tc, chunk=chunk, alpha=alpha)
    return pl.pallas_call(
        kern,
        grid=(bsz // nb, t // tc),
        in_specs=[
            pl.BlockSpec((nb, tc, d), lambda bi, ti: (bi, ti, 0)),
            pl.BlockSpec((nb, 6, d), lambda bi, ti: (bi, 0, 0)),
            pl.BlockSpec((nb, CONV_WIDTH - 1, d_conv), lambda bi, ti: (bi, 0, 0)),
            pl.BlockSpec((nb, heads, head_k, head_v), lambda bi, ti: (bi, 0, 0, 0)),
            resident(w_in.shape),
            pl.BlockSpec(w_conv.shape, const2),
            pl.BlockSpec(w_f.shape, const2),
            pl.BlockSpec(b_f.shape, const2),
            pl.BlockSpec(gain.shape, const2),
            resident(w_o.shape),
            pl.BlockSpec(ln_g.shape, const2),
            pl.BlockSpec(ln_b.shape, const2),
        ],
        out_specs=[
            pl.BlockSpec((nb, tc, d), lambda bi, ti: (bi, ti, 0)),
            pl.BlockSpec((nb, CONV_WIDTH - 1, d_conv), lambda bi, ti: (bi, 0, 0)),
            pl.BlockSpec((nb, heads, head_k, head_v), lambda bi, ti: (bi, 0, 0, 0)),
        ],
        out_shape=[
            jax.ShapeDtypeStruct((bsz, t, d), F32),
            jax.ShapeDtypeStruct(conv0.shape, F32),
            jax.ShapeDtypeStruct(s0.shape, F32),
        ],
        scratch_shapes=[
            pltpu.VMEM((rows, d), BF16),
            pltpu.VMEM((rows, d_conv), F32),
            pltpu.VMEM((rows, d_conv), F32),
            pltpu.VMEM((rows, gla_dk), F32),
            pltpu.VMEM((rows, gla_dk), F32),
            pltpu.VMEM((rows, d_gla), BF16),
            pltpu.VMEM((rows, d_gla), F32),
            pltpu.VMEM((rows, gla_dk), F32),
            pltpu.VMEM((rows, d), BF16),
        ],
        compiler_params=pltpu.CompilerParams(dimension_semantics=("arbitrary", "arbitrary"),
                                             vmem_limit_bytes=VMEM_LIMIT),
        name="mixer",
    )(x, mod, conv0, s0, w_in, w_conv, w_f, b_f, gain, w_o, ln_g, ln_b)


def _ffn_kernel(x1_ref, mod_ref, wg_ref, wu_ref, wd_ref, ln_g_ref, ln_b_ref, o_ref,
                h2_s, acc_s, *, nb, tc, alpha):
    d_model = x1_ref.shape[-1]
    rows = nb * tc
    f = pl.program_id(1)

    @pl.when(f == 0)
    def _():
        h2 = x1_ref[...] * (1.0 + mod_ref[:, 4:5, :]) + mod_ref[:, 3:4, :]
        h2_s[...] = h2.reshape(rows, d_model).astype(BF16)
        acc_s[...] = jnp.zeros_like(acc_s)

    h2 = h2_s[...]
    a = jnp.dot(h2, wg_ref[...], preferred_element_type=F32)
    b = jnp.dot(h2, wu_ref[...], preferred_element_type=F32)
    acc_s[...] += jnp.dot((_silu(a) * b).astype(BF16), wd_ref[...], preferred_element_type=F32)

    @pl.when(f == pl.num_programs(1) - 1)
    def _():
        y = alpha * x1_ref[...] + mod_ref[:, 5:6, :] * acc_s[...].reshape(nb, tc, d_model)
        o_ref[...] = _layernorm(y, ln_g_ref[...], ln_b_ref[...])


def _ffn(x1, mod, w_gate, w_up, w_down, ln_g, ln_b, *, nb, tc, tf, alpha):
    bsz, t, d = x1.shape
    d_ff = w_gate.shape[1]
    rows = nb * tc
    kern = functools.partial(_ffn_kernel, nb=nb, tc=tc, alpha=alpha)
    return pl.pallas_call(
        kern,
        grid=((bsz // nb) * (t // tc), d_ff // tf),
        in_specs=[
            pl.BlockSpec((nb, tc, d), lambda ri, fi: (ri // (t // tc), ri % (t // tc), 0)),
            pl.BlockSpec((nb, 6, d), lambda ri, fi: (ri // (t // tc), 0, 0)),
            pl.BlockSpec((d, tf), lambda ri, fi: (0, fi)),
            pl.BlockSpec((d, tf), lambda ri, fi: (0, fi)),
            pl.BlockSpec((tf, d), lambda ri, fi: (fi, 0)),
            pl.BlockSpec(ln_g.shape, lambda ri, fi: (0, 0)),
            pl.BlockSpec(ln_b.shape, lambda ri, fi: (0, 0)),
        ],
        out_specs=pl.BlockSpec((nb, tc, d), lambda ri, fi: (ri // (t // tc), ri % (t // tc), 0)),
        out_shape=jax.ShapeDtypeStruct((bsz, t, d), F32),
        scratch_shapes=[pltpu.VMEM((rows, d), BF16), pltpu.VMEM((rows, d), F32)],
        compiler_params=pltpu.CompilerParams(dimension_semantics=("arbitrary", "arbitrary"),
                                             vmem_limit_bytes=VMEM_LIMIT),
        name="ffn",
    )(x1, mod, w_gate, w_up, w_down, ln_g, ln_b)


def kernel(x_prompt, x_sample, c_prompt, c_sample, cache_conv, state_gla, w_ada, b_ada, w_in,
           w_conv, w_f, b_f, gla_gain, w_o, ln1_g, ln1_b, w_gate, w_up, w_down, ln2_g, ln2_b):
    depth = w_ada.shape[0]
    bp, _, d_model = x_prompt.shape
    bs, t_s, _ = x_sample.shape
    d_conv = cache_conv.shape[-1]
    heads, head_k, head_v = state_gla.shape[2:]
    gate_rank = w_f.shape[1]
    alpha = (2.0 * depth) ** 0.25

    yp, ys = x_prompt, x_sample
    conv_p, gla_p, conv_s, gla_s = [], [], [], []
    for l in range(depth):
        c_all = jnp.concatenate([c_prompt, c_sample], axis=0)
        pad_rows = -c_all.shape[0] % 8
        c_all = jnp.pad(c_all, ((0, pad_rows), (0, 0)))
        mod = _modulation(c_all, w_ada[l], b_ada[l], tn=1536)
        mod_p = mod[:bp].reshape(bp, 6, d_model)
        mod_s = mod[bp:bp + bs].reshape(bs, 6, d_model)

        w_in_b = jnp.pad(w_in[l].astype(BF16), ((0, 0), (0, LANE - gate_rank)))
        w_f_p = jnp.pad(w_f[l], ((0, LANE - gate_rank), (0, 0)))
        w_o_b = w_o[l].astype(BF16)
        wg_b, wu_b, wd_b = w_gate[l].astype(BF16), w_up[l].astype(BF16), w_down[l].astype(BF16)
        row = lambda a: a.reshape(1, -1)
        mixer_w = (w_in_b, w_conv[l], w_f_p, row(b_f[l]), row(gla_gain[l]), w_o_b,
                   row(ln1_g[l]), row(ln1_b[l]))
        ffn_w = (wg_b, wu_b, wd_b, row(ln2_g[l]), row(ln2_b[l]))

        conv0 = jnp.zeros((bp, CONV_WIDTH - 1, d_conv), F32)
        gla0 = jnp.zeros((bp, heads, head_k, head_v), F32)
        x1p, cp, sp = _mixer(yp, mod_p, conv0, gla0, *mixer_w,
                             nb=1, tc=256, chunk=PROMPT_CHUNK, alpha=alpha)
        yp = _ffn(x1p, mod_p, *ffn_w, nb=1, tc=512, tf=512, alpha=alpha)

        x1s, cs, ss = _mixer(ys, mod_s, cache_conv[l], state_gla[l], *mixer_w,
                             nb=4, tc=t_s, chunk=t_s, alpha=alpha)
        ys = _ffn(x1s, mod_s, *ffn_w, nb=bs, tc=t_s, tf=512, alpha=alpha)

        conv_p.append(cp)
        gla_p.append(sp)
        conv_s.append(cs)
        gla_s.append(ss)
    return (yp, ys, jnp.stack(conv_p), jnp.stack(gla_p), jnp.stack(conv_s), jnp.stack(gla_s))
```

```python
import functools

import jax
import jax.numpy as jnp
from jax import lax
from jax.experimental import pallas as pl
from jax.experimental.pallas import tpu as pltpu

F32 = jnp.float32
BF16 = jnp.bfloat16

CONV_WIDTH = 3
GLA_HEADS = 4
GATE_TAU = 16.0
LN_EPS = 1e-5
RMS_EPS = 1e-6
LANE = 128
VMEM_LIMIT = 60 * 1024 * 1024

PROMPT_MIXER = dict(nb=1, tc=256, chunk=64)
PROMPT_FFN = dict(nb=1, tc=1024, tf=512, sub=512)
MOD_TN = 1536
SAMPLE_MIXER_NB = 4
SAMPLE_FFN_TF = 512
LN_ROWS = 128


def _layernorm(y, g, b):
    mu = jnp.mean(y, axis=-1, keepdims=True)
    d = y - mu
    var = jnp.mean(d * d, axis=-1, keepdims=True)
    return d * lax.rsqrt(var + LN_EPS) * g + b


def _silu(a):
    return a * jax.nn.sigmoid(a)


def _mod_kernel(c_ref, w_ref, b_ref, o_ref):
    a = _silu(c_ref[...]).astype(BF16)
    o_ref[...] = jnp.dot(a, w_ref[...].astype(BF16), preferred_element_type=F32) + b_ref[...]


def _modulation(c_all, w_ada, b_ada, *, tn):
    m, d = c_all.shape
    n = w_ada.shape[1]
    return pl.pallas_call(
        _mod_kernel,
        grid=(n // tn,),
        in_specs=[pl.BlockSpec((m, d), lambda j: (0, 0)),
                  pl.BlockSpec((d, tn), lambda j: (0, j)),
                  pl.BlockSpec((1, tn), lambda j: (0, j))],
        out_specs=pl.BlockSpec((m, tn), lambda j: (0, j)),
        out_shape=jax.ShapeDtypeStruct((m, n), F32),
        compiler_params=pltpu.CompilerParams(dimension_semantics=("arbitrary",),
                                             vmem_limit_bytes=VMEM_LIMIT),
        name="adaln_modulation",
    )(c_all, w_ada, b_ada.reshape(1, n))


def _mixer_kernel(x_ref, mod_ref, conv0_ref, s0_ref, w_in_ref, w_conv_ref, w_f_ref, b_f_ref,
                  gain_ref, w_o_ref, ln_g_ref, ln_b_ref,
                  x1_ref, conv_ref, s_ref,
                  h_s, hb_s, u_s, qe_s, ke_s, kd_s, v_s, go_s, st_s, mix_s,
                  *, nb, tc, chunk, alpha):
    d_model = x_ref.shape[-1]
    d_conv = conv_ref.shape[-1]
    head_k, head_v = s_ref.shape[-2], s_ref.shape[-1]
    gla_dk = GLA_HEADS * head_k
    d_gla = GLA_HEADS * head_v
    rows = nb * tc
    n_chunks = tc // chunk
    tot_chunks = rows // chunk
    contract0 = (((0,), (0,)), ((), ()))
    contract1 = (((1,), (1,)), ((), ()))

    @pl.when(pl.program_id(1) == 0)
    def _():
        conv_ref[...] = conv0_ref[...]
        for i in range(nb):
            for hd in range(GLA_HEADS):
                st_s[i, hd] = s0_ref[i, hd].T

    h = x_ref[...] * (1.0 + mod_ref[:, 1:2, :]) + mod_ref[:, 0:1, :]
    h_s[...] = h.reshape(rows, d_model).astype(BF16)

    def proj(lo, width):
        return jnp.dot(h_s[...], w_in_ref[:, lo:lo + width], preferred_element_type=F32)

    o_hb, o_hc, o_hin = 0, d_conv, 2 * d_conv
    o_q = 3 * d_conv
    o_k = o_q + gla_dk
    o_v = o_k + gla_dk
    o_go = o_v + d_gla
    o_f = o_go + d_gla

    z = jnp.dot(proj(o_f, LANE), w_f_ref[...], preferred_element_type=F32,
                precision=lax.Precision.HIGHEST) + b_f_ref[...]
    lf = jax.nn.log_sigmoid(z) * (1.0 / GATE_TAU)
    lf1 = lf.astype(BF16)
    rem = lf - lf1.astype(F32)
    lf2 = rem.astype(BF16)
    lf3 = (rem - lf2.astype(F32)).astype(BF16)
    ri = lax.broadcasted_iota(jnp.int32, (rows, rows), 0)
    ci = lax.broadcasted_iota(jnp.int32, (rows, rows), 1)
    ltri = ((ri // chunk == ci // chunk) & (ci <= ri)).astype(BF16)
    parts = jnp.dot(ltri, jnp.concatenate([lf1, lf2, lf3], axis=1), preferred_element_type=F32)
    b = parts[:, 0:gla_dk] + parts[:, gla_dk:2 * gla_dk] + parts[:, 2 * gla_dk:3 * gla_dk]
    b_end = [b[(n + 1) * chunk - 1:(n + 1) * chunk, :] for n in range(tot_chunks)]
    b_tot = jnp.concatenate([jnp.broadcast_to(e, (chunk, gla_dk)) for e in b_end], axis=0)
    chunk_decay = [jnp.exp(e) for e in b_end]

    q = proj(o_q, gla_dk) * (head_k ** -0.5)
    k = proj(o_k, gla_dk)
    qe_s[...] = (q * jnp.exp(b)).astype(BF16)
    ke_s[...] = (k * jnp.exp(-b)).astype(BF16)
    kd_s[...] = (k * jnp.exp(b_tot - b)).astype(BF16)
    v_s[...] = proj(o_v, d_gla).astype(BF16)
    go_s[...] = proj(o_go, d_gla)

    cr = lax.broadcasted_iota(jnp.int32, (chunk, chunk), 0)
    cc = lax.broadcasted_iota(jnp.int32, (chunk, chunk), 1)
    causal = cr >= cc
    gain = gain_ref[...]

    def gla_chunk(n):
        i = n // n_chunks
        rs = slice(n * chunk, (n + 1) * chunk)
        for hd in range(GLA_HEADS):
            ks = slice(hd * head_k, (hd + 1) * head_k)
            vs = slice(hd * head_v, (hd + 1) * head_v)
            qe, ke, kd, v = qe_s[rs, ks], ke_s[rs, ks], kd_s[rs, ks], v_s[rs, vs]
            st = st_s[i, hd]
            a = lax.dot_general(qe, ke, contract1, preferred_element_type=F32)
            a = jnp.where(causal, a, 0.0).astype(BF16)
            o = (jnp.dot(a, v, preferred_element_type=F32)
                 + lax.dot_general(qe, st.astype(BF16), contract1, preferred_element_type=F32))
            st_s[i, hd] = st * chunk_decay[n][:, ks] + lax.dot_general(
                v, kd, contract0, preferred_element_type=F32)
            o = o * lax.rsqrt(jnp.mean(o * o, axis=-1, keepdims=True) + RMS_EPS) * gain
            mix_s[rs, d_conv + hd * head_v:d_conv + (hd + 1) * head_v] = (
                o * _silu(go_s[rs, vs])).astype(BF16)

    row = lax.broadcasted_iota(jnp.int32, (tc, d_conv), 0)
    w0, w1, w2 = w_conv_ref[0:1, :], w_conv_ref[1:2, :], w_conv_ref[2:3, :]

    def conv_segment(i):
        rs = slice(i * tc, (i + 1) * tc)
        u = u_s[rs, :]
        st = conv_ref[i]
        p0, p1 = st[0:1, :], st[1:2, :]
        u1 = jnp.where(row == 0, p1, pltpu.roll(u, 1, axis=0))
        u2 = jnp.where(row == 0, p0, jnp.where(row == 1, p1, pltpu.roll(u, 2, axis=0)))
        conv = u2 * w0 + u1 * w1 + u * w2
        mix_s[rs, 0:d_conv] = (hb_s[rs, :] * conv).astype(BF16)
        conv_ref[i] = u[tc - (CONV_WIDTH - 1):tc, :]

    gla_chunk(0)
    hb_s[...] = proj(o_hb, d_conv)
    for n in range(1, tot_chunks - 1):
        gla_chunk(n)
    u_s[...] = proj(o_hc, d_conv) * proj(o_hin, d_conv)
    for n in range(max(tot_chunks - 1, 1), tot_chunks):
        gla_chunk(n)
    for i in range(nb):
        conv_segment(i)

    @pl.when(pl.program_id(1) == pl.num_programs(1) - 1)
    def _():
        for i in range(nb):
            for hd in range(GLA_HEADS):
                s_ref[i, hd] = st_s[i, hd].T

    mo = jnp.dot(mix_s[...], w_o_ref[...], preferred_element_type=F32)
    y = alpha * x_ref[...] + mod_ref[:, 2:3, :] * mo.reshape(nb, tc, d_model)
    x1_ref[...] = _layernorm(y, ln_g_ref[...], ln_b_ref[...])


def _mixer(x, mod, conv0, s0, w_in, w_conv, w_f, b_f, gain, w_o, ln_g, ln_b,
           *, nb, tc, chunk, alpha):
    bsz, t, d = x.shape
    d_conv = conv0.shape[-1]
    heads, head_k, head_v = s0.shape[1:]
    gla_dk, d_gla = heads * head_k, heads * head_v
    rows = nb * tc
    const2 = lambda bi, ti: (0, 0)
    resident = functools.partial(pl.BlockSpec, index_map=const2, pipeline_mode=pl.Buffered(1))
    kern = functools.partial(_mixer_kernel, nb=nb, tc=tc, chunk=chunk, alpha=alpha)
    return pl.pallas_call(
        kern,
        grid=(bsz // nb, t // tc),
        in_specs=[
            pl.BlockSpec((nb, tc, d), lambda bi, ti: (bi, ti, 0)),
            pl.BlockSpec((nb, 6, d), lambda bi, ti: (bi, 0, 0)),
            pl.BlockSpec((nb, CONV_WIDTH - 1, d_conv), lambda bi, ti: (bi, 0, 0)),
            pl.BlockSpec((nb, heads, head_k, head_v), lambda bi, ti: (bi, 0, 0, 0)),
            resident(w_in.shape),
            pl.BlockSpec(w_conv.shape, const2),
            pl.BlockSpec(w_f.shape, const2),
            pl.BlockSpec(b_f.shape, const2),
            pl.BlockSpec(gain.shape, const2),
            resident(w_o.shape),
            pl.BlockSpec(ln_g.shape, const2),
            pl.BlockSpec(ln_b.shape, const2),
        ],
        out_specs=[
            pl.BlockSpec((nb, tc, d), lambda bi, ti: (bi, ti, 0)),
            pl.BlockSpec((nb, CONV_WIDTH - 1, d_conv), lambda bi, ti: (bi, 0, 0)),
            pl.BlockSpec((nb, heads, head_k, head_v), lambda bi, ti: (bi, 0, 0, 0)),
        ],
        out_shape=[
            jax.ShapeDtypeStruct((bsz, t, d), F32),
            jax.ShapeDtypeStruct(conv0.shape, F32),
            jax.ShapeDtypeStruct(s0.shape, F32),
        ],
        scratch_shapes=[
            pltpu.VMEM((rows, d), BF16),
            pltpu.VMEM((rows, d_conv), F32),
            pltpu.VMEM((rows, d_conv), F32),
            pltpu.VMEM((rows, gla_dk), BF16),
            pltpu.VMEM((rows, gla_dk), BF16),
            pltpu.VMEM((rows, gla_dk), BF16),
            pltpu.VMEM((rows, d_gla), BF16),
            pltpu.VMEM((rows, d_gla), F32),
            pltpu.VMEM((nb, heads, head_v, head_k), F32),
            pltpu.VMEM((rows, d), BF16),
        ],
        compiler_params=pltpu.CompilerParams(dimension_semantics=("arbitrary", "arbitrary"),
                                             vmem_limit_bytes=VMEM_LIMIT),
        name="mixer",
    )(x, mod, conv0, s0, w_in, w_conv, w_f, b_f, gain, w_o, ln_g, ln_b)


def _ffn_kernel(x1_ref, mod_ref, wg_ref, wu_ref, wd_ref, ln_g_ref, ln_b_ref, o_ref,
                h2_s, *, nb, tc, sub, alpha):
    d_model = x1_ref.shape[-1]
    rows = nb * tc
    f = pl.program_id(1)

    @pl.when(f == 0)
    def _():
        h2 = x1_ref[...] * (1.0 + mod_ref[:, 4:5, :]) + mod_ref[:, 3:4, :]
        h2_s[...] = h2.reshape(rows, d_model).astype(BF16)
        o_ref[...] = jnp.zeros_like(o_ref)

    def swiglu(h2):
        a = jnp.dot(h2, wg_ref[...], preferred_element_type=F32)
        b = jnp.dot(h2, wu_ref[...], preferred_element_type=F32)
        return jnp.dot((_silu(a) * b).astype(BF16), wd_ref[...], preferred_element_type=F32)

    if nb == 1:
        for r in range(0, tc, sub):
            o_ref[0, r:r + sub, :] += swiglu(h2_s[r:r + sub, :])
    else:
        o_ref[...] += swiglu(h2_s[...]).reshape(nb, tc, d_model)

    @pl.when(f == pl.num_programs(1) - 1)
    def _():
        ln_rows = min(tc, LN_ROWS)
        for i in range(nb):
            for r in range(0, tc, ln_rows):
                y = alpha * x1_ref[i, r:r + ln_rows, :] + mod_ref[i, 5:6, :] * o_ref[i, r:r + ln_rows, :]
                o_ref[i, r:r + ln_rows, :] = _layernorm(y, ln_g_ref[...], ln_b_ref[...])


def _ffn(x1, mod, w_gate, w_up, w_down, ln_g, ln_b, *, nb, tc, tf, sub, alpha):
    bsz, t, d = x1.shape
    d_ff = w_gate.shape[1]
    rows = nb * tc
    t_tiles = t // tc
    kern = functools.partial(_ffn_kernel, nb=nb, tc=tc, sub=sub, alpha=alpha)
    x_map = lambda ri, fi: (ri // t_tiles, ri % t_tiles, 0)
    return pl.pallas_call(
        kern,
        grid=((bsz // nb) * t_tiles, d_ff // tf),
        in_specs=[
            pl.BlockSpec((nb, tc, d), x_map),
            pl.BlockSpec((nb, 6, d), lambda ri, fi: (ri // t_tiles, 0, 0)),
            pl.BlockSpec((d, tf), lambda ri, fi: (0, fi)),
            pl.BlockSpec((d, tf), lambda ri, fi: (0, fi)),
            pl.BlockSpec((tf, d), lambda ri, fi: (fi, 0)),
            pl.BlockSpec(ln_g.shape, lambda ri, fi: (0, 0)),
            pl.BlockSpec(ln_b.shape, lambda ri, fi: (0, 0)),
        ],
        out_specs=pl.BlockSpec((nb, tc, d), x_map),
        out_shape=jax.ShapeDtypeStruct((bsz, t, d), F32),
        scratch_shapes=[pltpu.VMEM((rows, d), BF16)],
        compiler_params=pltpu.CompilerParams(dimension_semantics=("arbitrary", "arbitrary"),
                                             vmem_limit_bytes=VMEM_LIMIT),
        name="ffn",
    )(x1, mod, w_gate, w_up, w_down, ln_g, ln_b)


def kernel(x_prompt, x_sample, c_prompt, c_sample, cache_conv, state_gla, w_ada, b_ada, w_in,
           w_conv, w_f, b_f, gla_gain, w_o, ln1_g, ln1_b, w_gate, w_up, w_down, ln2_g, ln2_b):
    depth = w_ada.shape[0]
    bp, _, d_model = x_prompt.shape
    bs, t_s, _ = x_sample.shape
    d_conv = cache_conv.shape[-1]
    heads, head_k, head_v = state_gla.shape[2:]
    gate_rank = w_f.shape[1]
    alpha = (2.0 * depth) ** 0.25

    yp, ys = x_prompt, x_sample
    conv_p, gla_p, conv_s, gla_s = [], [], [], []
    for l in range(depth):
        c_all = jnp.concatenate([c_prompt, c_sample], axis=0)
        pad_rows = -c_all.shape[0] % 8
        c_all = jnp.pad(c_all, ((0, pad_rows), (0, 0)))
        mod = _modulation(c_all, w_ada[l], b_ada[l], tn=MOD_TN)
        mod_p = mod[:bp].reshape(bp, 6, d_model)
        mod_s = mod[bp:bp + bs].reshape(bs, 6, d_model)

        w_in_b = jnp.pad(w_in[l].astype(BF16), ((0, 0), (0, LANE - gate_rank)))
        w_f_p = jnp.pad(w_f[l], ((0, LANE - gate_rank), (0, 0)))
        w_o_b = w_o[l].astype(BF16)
        wg_b, wu_b, wd_b = w_gate[l].astype(BF16), w_up[l].astype(BF16), w_down[l].astype(BF16)
        row = lambda a: a.reshape(1, -1)
        mixer_w = (w_in_b, w_conv[l], w_f_p, row(b_f[l]), row(gla_gain[l]), w_o_b,
                   row(ln1_g[l]), row(ln1_b[l]))
        ffn_w = (wg_b, wu_b, wd_b, row(ln2_g[l]), row(ln2_b[l]))

        conv0 = jnp.zeros((bp, CONV_WIDTH - 1, d_conv), F32)
        gla0 = jnp.zeros((bp, heads, head_k, head_v), F32)
        x1p, cp, sp = _mixer(yp, mod_p, conv0, gla0, *mixer_w, alpha=alpha, **PROMPT_MIXER)
        yp = _ffn(x1p, mod_p, *ffn_w, alpha=alpha, **PROMPT_FFN)

        x1s, cs, ss = _mixer(ys, mod_s, cache_conv[l], state_gla[l], *mixer_w,
                             nb=SAMPLE_MIXER_NB, tc=t_s, chunk=t_s, alpha=alpha)
        ys = _ffn(x1s, mod_s, *ffn_w, nb=bs, tc=t_s, tf=SAMPLE_FFN_TF, sub=t_s, alpha=alpha)

        conv_p.append(cp)
        gla_p.append(sp)
        conv_s.append(cs)
        gla_s.append(ss)
    return (yp, ys, jnp.stack(conv_p), jnp.stack(gla_p), jnp.stack(conv_s), jnp.stack(gla_s))
```

```python
import functools

import jax
import jax.numpy as jnp
from jax import lax
from jax.experimental import pallas as pl
from jax.experimental.pallas import tpu as pltpu

F32 = jnp.float32
BF16 = jnp.bfloat16

CONV_WIDTH = 3
GLA_HEADS = 4
GATE_TAU = 16.0
LN_EPS = 1e-5
RMS_EPS = 1e-6
LANE = 128
VMEM_LIMIT = 60 * 1024 * 1024

PROMPT_MIXER = dict(nb=1, tc=256, chunk=64)
PROMPT_FFN = dict(nb=1, tc=1024, tf=512, sub=512)
MOD_TN = 1536
SAMPLE_MIXER_NB = 4
SAMPLE_FFN_TF = 512
LN_ROWS = 128
GATE_LHS_TERM = (0, 1, 2, 0, 1, 0)
GATE_RHS_TERM = (0, 0, 0, 1, 1, 2)


def _bf16_terms(a):
    t1 = a.astype(BF16)
    r1 = a - t1.astype(F32)
    t2 = r1.astype(BF16)
    t3 = (r1 - t2.astype(F32)).astype(BF16)
    return t1, t2, t3


def _layernorm(y, g, b):
    mu = jnp.mean(y, axis=-1, keepdims=True)
    d = y - mu
    var = jnp.mean(d * d, axis=-1, keepdims=True)
    return d * lax.rsqrt(var + LN_EPS) * g + b


def _silu(a):
    return a * jax.nn.sigmoid(a)


def _mod_kernel(c_ref, w_ref, b_ref, o_ref):
    a = _silu(c_ref[...]).astype(BF16)
    o_ref[...] = jnp.dot(a, w_ref[...].astype(BF16), preferred_element_type=F32) + b_ref[...]


def _modulation(c_all, w_ada, b_ada, *, tn):
    m, d = c_all.shape
    n = w_ada.shape[1]
    return pl.pallas_call(
        _mod_kernel,
        grid=(n // tn,),
        in_specs=[pl.BlockSpec((m, d), lambda j: (0, 0)),
                  pl.BlockSpec((d, tn), lambda j: (0, j)),
                  pl.BlockSpec((1, tn), lambda j: (0, j))],
        out_specs=pl.BlockSpec((m, tn), lambda j: (0, j)),
        out_shape=jax.ShapeDtypeStruct((m, n), F32),
        compiler_params=pltpu.CompilerParams(dimension_semantics=("arbitrary",),
                                             vmem_limit_bytes=VMEM_LIMIT),
        name="adaln_modulation",
    )(c_all, w_ada, b_ada.reshape(1, n))


def _mixer_kernel(x_ref, mod_ref, conv0_ref, s0_ref, w_in_ref, w_conv_ref, w_f_ref, b_f_ref,
                  gain_ref, w_o_ref, ln_g_ref, ln_b_ref,
                  x1_ref, conv_ref, s_ref,
                  h_s, hb_s, u_s, qe_s, ke_s, kd_s, v_s, go_s, st_s, a_s, kv_s, stb_s, mix_s,
                  *, nb, tc, chunk, gate_rank, alpha):
    d_model = x_ref.shape[-1]
    d_conv = conv_ref.shape[-1]
    head_k, head_v = s_ref.shape[-2], s_ref.shape[-1]
    gla_dk = GLA_HEADS * head_k
    d_gla = GLA_HEADS * head_v
    rows = nb * tc
    n_chunks = tc // chunk
    tot_chunks = rows // chunk
    contract0 = (((0,), (0,)), ((), ()))
    contract1 = (((1,), (1,)), ((), ()))

    @pl.when(pl.program_id(1) == 0)
    def _():
        conv_ref[...] = conv0_ref[...]
        for i in range(nb):
            for hd in range(GLA_HEADS):
                st_s[i, hd] = s0_ref[i, hd].T

    h = x_ref[...] * (1.0 + mod_ref[:, 1]) + mod_ref[:, 0]
    h_s[...] = h.reshape(rows, d_model).astype(BF16)

    def proj(lo, width):
        return jnp.dot(h_s[...], w_in_ref[:, lo:lo + width], preferred_element_type=F32)

    o_hb, o_hc, o_hin = 0, d_conv, 2 * d_conv
    o_q = 3 * d_conv
    o_k = o_q + gla_dk
    o_v = o_k + gla_dk
    o_go = o_v + d_gla
    o_f = o_go + d_gla

    flr = proj(o_f, LANE)
    q = proj(o_q, gla_dk) * (head_k ** -0.5)
    k = proj(o_k, gla_dk)
    t1 = flr.astype(BF16).astype(F32)
    t2 = (flr - t1).astype(BF16).astype(F32)
    t3 = (flr - t1 - t2).astype(BF16).astype(F32)
    terms = (t1, t2, t3)
    grp = lax.broadcasted_iota(jnp.int32, flr.shape, 1) // gate_rank
    lhs = terms[0]
    for g, term in enumerate(GATE_LHS_TERM):
        if term:
            lhs = jnp.where(grp == g, terms[term], lhs)
    z = jnp.dot(lhs.astype(BF16), w_f_ref[...], preferred_element_type=F32) + b_f_ref[...]
    lf = jax.nn.log_sigmoid(z) * (1.0 / GATE_TAU)
    lf1 = lf.astype(BF16)
    rem = lf - lf1.astype(F32)
    lf2 = rem.astype(BF16)
    lf3 = (rem - lf2.astype(F32)).astype(BF16)
    ri = lax.broadcasted_iota(jnp.int32, (rows, rows), 0)
    ci = lax.broadcasted_iota(jnp.int32, (rows, rows), 1)
    ltri = ((ri // chunk == ci // chunk) & (ci <= ri)).astype(BF16)
    parts = jnp.dot(ltri, jnp.concatenate([lf1, lf2, lf3], axis=1), preferred_element_type=F32)
    b = parts[:, 0:gla_dk] + parts[:, gla_dk:2 * gla_dk] + parts[:, 2 * gla_dk:3 * gla_dk]
    b_end = [b[(n + 1) * chunk - 1:(n + 1) * chunk, :] for n in range(tot_chunks)]
    b_tot = jnp.concatenate([jnp.broadcast_to(e, (chunk, gla_dk)) for e in b_end], axis=0)
    chunk_decay = [jnp.exp(e) for e in b_end]

    qe_s[...] = (q * jnp.exp(b)).astype(BF16)
    ke_s[...] = (k * jnp.exp(-b)).astype(BF16)
    kd_s[...] = (k * jnp.exp(b_tot - b)).astype(BF16)
    v_s[...] = proj(o_v, d_gla).astype(BF16)
    go_s[...] = proj(o_go, d_gla)

    cr = lax.broadcasted_iota(jnp.int32, (chunk, chunk), 0)
    cc = lax.broadcasted_iota(jnp.int32, (chunk, chunk), 1)
    causal = cr >= cc
    gain = gain_ref[...]

    head_chunks = [(n, hd) for n in range(tot_chunks) for hd in range(GLA_HEADS)]

    def slices(n, hd):
        return (slice(n * chunk, (n + 1) * chunk), slice(hd * head_k, (hd + 1) * head_k),
                slice(hd * head_v, (hd + 1) * head_v))

    def gla_scores_and_updates():
        for n, hd in head_chunks:
            rs, ks, vs = slices(n, hd)
            a = lax.dot_general(qe_s[rs, ks], ke_s[rs, ks], contract1, preferred_element_type=F32)
            a_s[rs, hd * chunk:(hd + 1) * chunk] = jnp.where(causal, a, 0.0).astype(BF16)
            kv_s[n, hd] = lax.dot_general(v_s[rs, vs], kd_s[rs, ks], contract0,
                                          preferred_element_type=F32)

    def gla_state_recurrence():
        for n, hd in head_chunks:
            _, ks, _ = slices(n, hd)
            st = st_s[n // n_chunks, hd]
            stb_s[n, hd] = st.astype(BF16)
            st_s[n // n_chunks, hd] = st * chunk_decay[n][:, ks] + kv_s[n, hd]

    def gla_outputs():
        for n, hd in head_chunks:
            rs, ks, vs = slices(n, hd)
            o = (jnp.dot(a_s[rs, hd * chunk:(hd + 1) * chunk], v_s[rs, vs],
                         preferred_element_type=F32)
                 + lax.dot_general(qe_s[rs, ks], stb_s[n, hd], contract1,
                                   preferred_element_type=F32))
            o = o * lax.rsqrt(jnp.mean(o * o, axis=-1, keepdims=True) + RMS_EPS) * gain
            mix_s[rs, d_conv + hd * head_v:d_conv + (hd + 1) * head_v] = (
                o * _silu(go_s[rs, vs])).astype(BF16)

    row = lax.broadcasted_iota(jnp.int32, (tc, d_conv), 0)
    w0, w1, w2 = w_conv_ref[0:1, :], w_conv_ref[1:2, :], w_conv_ref[2:3, :]

    def conv_segment(i):
        rs = slice(i * tc, (i + 1) * tc)
        u = u_s[rs, :]
        st = conv_ref[i]
        p0, p1 = st[0:1, :], st[1:2, :]
        u1 = jnp.where(row == 0, p1, pltpu.roll(u, 1, axis=0))
        u2 = jnp.where(row == 0, p0, jnp.where(row == 1, p1, pltpu.roll(u, 2, axis=0)))
        conv = u2 * w0 + u1 * w1 + u * w2
        mix_s[rs, 0:d_conv] = (hb_s[rs, :] * conv).astype(BF16)
        conv_ref[i] = u[tc - (CONV_WIDTH - 1):tc, :]

    gla_scores_and_updates()
    u_s[...] = proj(o_hc, d_conv) * proj(o_hin, d_conv)
    gla_state_recurrence()
    hb_s[...] = proj(o_hb, d_conv)
    for i in range(nb):
        conv_segment(i)
    mo = jnp.dot(mix_s[:, 0:d_conv], w_o_ref[0:d_conv, :], preferred_element_type=F32)
    gla_outputs()
    mo = mo + jnp.dot(mix_s[:, d_conv:d_model], w_o_ref[d_conv:d_model, :],
                      preferred_element_type=F32)
    y = alpha * x_ref[...] + mod_ref[:, 2] * mo.reshape(nb, tc, d_model)
    x1_ref[...] = _layernorm(y, ln_g_ref[...], ln_b_ref[...])

    @pl.when(pl.program_id(1) == pl.num_programs(1) - 1)
    def _():
        for i in range(nb):
            for hd in range(GLA_HEADS):
                s_ref[i, hd] = st_s[i, hd].T


def _mixer(x, mod, conv0, s0, w_in, w_conv, w_f, b_f, gain, w_o, ln_g, ln_b,
           *, nb, tc, chunk, gate_rank, alpha):
    bsz, t, d = x.shape
    d_conv = conv0.shape[-1]
    heads, head_k, head_v = s0.shape[1:]
    gla_dk, d_gla = heads * head_k, heads * head_v
    rows = nb * tc
    const2 = lambda bi, ti: (0, 0)
    resident = functools.partial(pl.BlockSpec, index_map=const2, pipeline_mode=pl.Buffered(1))
    kern = functools.partial(_mixer_kernel, nb=nb, tc=tc, chunk=chunk, gate_rank=gate_rank,
                             alpha=alpha)
    return pl.pallas_call(
        kern,
        grid=(bsz // nb, t // tc),
        in_specs=[
            pl.BlockSpec((nb, tc, d), lambda bi, ti: (bi, ti, 0)),
            pl.BlockSpec((nb, 6, 1, d), lambda bi, ti: (bi, 0, 0, 0)),
            pl.BlockSpec((nb, CONV_WIDTH - 1, d_conv), lambda bi, ti: (bi, 0, 0)),
            pl.BlockSpec((nb, heads, head_k, head_v), lambda bi, ti: (bi, 0, 0, 0)),
            resident(w_in.shape),
            pl.BlockSpec(w_conv.shape, const2),
            pl.BlockSpec(w_f.shape, const2),
            pl.BlockSpec(b_f.shape, const2),
            pl.BlockSpec(gain.shape, const2),
            resident(w_o.shape),
            pl.BlockSpec(ln_g.shape, const2),
            pl.BlockSpec(ln_b.shape, const2),
        ],
        out_specs=[
            pl.BlockSpec((nb, tc, d), lambda bi, ti: (bi, ti, 0)),
            pl.BlockSpec((nb, CONV_WIDTH - 1, d_conv), lambda bi, ti: (bi, 0, 0)),
            pl.BlockSpec((nb, heads, head_k, head_v), lambda bi, ti: (bi, 0, 0, 0)),
        ],
        out_shape=[
            jax.ShapeDtypeStruct((bsz, t, d), F32),
            jax.ShapeDtypeStruct(conv0.shape, F32),
            jax.ShapeDtypeStruct(s0.shape, F32),
        ],
        scratch_shapes=[
            pltpu.VMEM((rows, d), BF16),
            pltpu.VMEM((rows, d_conv), F32),
            pltpu.VMEM((rows, d_conv), F32),
            pltpu.VMEM((rows, gla_dk), BF16),
            pltpu.VMEM((rows, gla_dk), BF16),
            pltpu.VMEM((rows, gla_dk), BF16),
            pltpu.VMEM((rows, d_gla), BF16),
            pltpu.VMEM((rows, d_gla), F32),
            pltpu.VMEM((nb, heads, head_v, head_k), F32),
            pltpu.VMEM((rows, heads * chunk), BF16),
            pltpu.VMEM((rows // chunk, heads, head_v, head_k), F32),
            pltpu.VMEM((rows // chunk, heads, head_v, head_k), BF16),
            pltpu.VMEM((rows, d), BF16),
        ],
        compiler_params=pltpu.CompilerParams(dimension_semantics=("arbitrary", "arbitrary"),
                                             vmem_limit_bytes=VMEM_LIMIT),
        name="mixer",
    )(x, mod, conv0, s0, w_in, w_conv, w_f, b_f, gain, w_o, ln_g, ln_b)


def _ffn_kernel(x1_ref, mod_ref, wg_ref, wu_ref, wd_ref, ln_g_ref, ln_b_ref, o_ref,
                h2_s, *, nb, tc, sub, alpha):
    d_model = x1_ref.shape[-1]
    rows = nb * tc
    f = pl.program_id(1)

    @pl.when(f == 0)
    def _():
        h2 = x1_ref[...] * (1.0 + mod_ref[:, 4]) + mod_ref[:, 3]
        h2_s[...] = h2.reshape(rows, d_model).astype(BF16)
        o_ref[...] = jnp.zeros_like(o_ref)

    def swiglu(h2):
        a = jnp.dot(h2, wg_ref[...], preferred_element_type=F32)
        b = jnp.dot(h2, wu_ref[...], preferred_element_type=F32)
        return jnp.dot((_silu(a) * b).astype(BF16), wd_ref[...], preferred_element_type=F32)

    if nb == 1:
        for r in range(0, tc, sub):
            o_ref[0, r:r + sub, :] += swiglu(h2_s[r:r + sub, :])
    else:
        o_ref[...] += swiglu(h2_s[...]).reshape(nb, tc, d_model)

    @pl.when(f == pl.num_programs(1) - 1)
    def _():
        ln_rows = min(tc, LN_ROWS)
        for i in range(nb):
            for r in range(0, tc, ln_rows):
                y = (alpha * x1_ref[i, r:r + ln_rows, :]
                     + mod_ref[i, 5] * o_ref[i, r:r + ln_rows, :])
                o_ref[i, r:r + ln_rows, :] = _layernorm(y, ln_g_ref[...], ln_b_ref[...])


def _ffn(x1, mod, w_gate, w_up, w_down, ln_g, ln_b, *, nb, tc, tf, sub, alpha):
    bsz, t, d = x1.shape
    d_ff = w_gate.shape[1]
    rows = nb * tc
    t_tiles = t // tc
    kern = functools.partial(_ffn_kernel, nb=nb, tc=tc, sub=sub, alpha=alpha)
    x_map = lambda ri, fi: (ri // t_tiles, ri % t_tiles, 0)
    return pl.pallas_call(
        kern,
        grid=((bsz // nb) * t_tiles, d_ff // tf),
        in_specs=[
            pl.BlockSpec((nb, tc, d), x_map),
            pl.BlockSpec((nb, 6, 1, d), lambda ri, fi: (ri // t_tiles, 0, 0, 0)),
            pl.BlockSpec((d, tf), lambda ri, fi: (0, fi)),
            pl.BlockSpec((d, tf), lambda ri, fi: (0, fi)),
            pl.BlockSpec((tf, d), lambda ri, fi: (fi, 0)),
            pl.BlockSpec(ln_g.shape, lambda ri, fi: (0, 0)),
            pl.BlockSpec(ln_b.shape, lambda ri, fi: (0, 0)),
        ],
        out_specs=pl.BlockSpec((nb, tc, d), x_map),
        out_shape=jax.ShapeDtypeStruct((bsz, t, d), F32),
        scratch_shapes=[pltpu.VMEM((rows, d), BF16)],
        compiler_params=pltpu.CompilerParams(dimension_semantics=("arbitrary", "arbitrary"),
                                             vmem_limit_bytes=VMEM_LIMIT),
        name="ffn",
    )(x1, mod, w_gate, w_up, w_down, ln_g, ln_b)


def kernel(x_prompt, x_sample, c_prompt, c_sample, cache_conv, state_gla, w_ada, b_ada, w_in,
           w_conv, w_f, b_f, gla_gain, w_o, ln1_g, ln1_b, w_gate, w_up, w_down, ln2_g, ln2_b):
    depth = w_ada.shape[0]
    bp, _, d_model = x_prompt.shape
    bs, t_s, _ = x_sample.shape
    d_conv = cache_conv.shape[-1]
    heads, head_k, head_v = state_gla.shape[2:]
    gate_rank = w_f.shape[1]
    alpha = (2.0 * depth) ** 0.25

    yp, ys = x_prompt, x_sample
    conv_p, gla_p, conv_s, gla_s = [], [], [], []
    for l in range(depth):
        c_all = jnp.concatenate([c_prompt, c_sample], axis=0)
        pad_rows = -c_all.shape[0] % 8
        c_all = jnp.pad(c_all, ((0, pad_rows), (0, 0)))
        mod = _modulation(c_all, w_ada[l], b_ada[l], tn=MOD_TN)
        mod_p = mod[:bp].reshape(bp, 6, 1, d_model)
        mod_s = mod[bp:bp + bs].reshape(bs, 6, 1, d_model)

        n_main = w_in.shape[-1] - gate_rank
        copies = len(GATE_LHS_TERM)
        fill = LANE - copies * gate_rank
        w_flr = w_in[l][:, n_main:]
        w_in_b = jnp.concatenate([w_in[l][:, :n_main]] + [w_flr] * copies
                                 + [jnp.zeros((d_model, fill), F32)], axis=1).astype(BF16)
        wf_terms = _bf16_terms(w_f[l])
        w_f_p = jnp.concatenate([wf_terms[t] for t in GATE_RHS_TERM]
                                + [jnp.zeros((fill, w_f.shape[-1]), BF16)], axis=0)
        w_o_b = w_o[l].astype(BF16)
        wg_b, wu_b, wd_b = w_gate[l].astype(BF16), w_up[l].astype(BF16), w_down[l].astype(BF16)
        row = lambda a: a.reshape(1, -1)
        mixer_w = (w_in_b, w_conv[l], w_f_p, row(b_f[l]), row(gla_gain[l]), w_o_b,
                   row(ln1_g[l]), row(ln1_b[l]))
        ffn_w = (wg_b, wu_b, wd_b, row(ln2_g[l]), row(ln2_b[l]))

        conv0 = jnp.zeros((bp, CONV_WIDTH - 1, d_conv), F32)
        gla0 = jnp.zeros((bp, heads, head_k, head_v), F32)
        x1p, cp, sp = _mixer(yp, mod_p, conv0, gla0, *mixer_w, gate_rank=gate_rank, alpha=alpha,
                             **PROMPT_MIXER)
        yp = _ffn(x1p, mod_p, *ffn_w, alpha=alpha, **PROMPT_FFN)

        x1s, cs, ss = _mixer(ys, mod_s, cache_conv[l], state_gla[l], *mixer_w,
                             nb=SAMPLE_MIXER_NB, tc=t_s, chunk=t_s, gate_rank=gate_rank,
                             alpha=alpha)
        ys = _ffn(x1s, mod_s, *ffn_w, nb=bs, tc=t_s, tf=SAMPLE_FFN_TF, sub=t_s, alpha=alpha)

        conv_p.append(cp)
        gla_p.append(sp)
        conv_s.append(cs)
        gla_s.append(ss)
    return (yp, ys, jnp.stack(conv_p), jnp.stack(gla_p), jnp.stack(conv_s), jnp.stack(gla_s))
```

```python
import functools

import jax
import jax.numpy as jnp
from jax import lax
from jax.experimental import pallas as pl
from jax.experimental.pallas import tpu as pltpu

F32 = jnp.float32
BF16 = jnp.bfloat16

CONV_WIDTH = 3
GLA_HEADS = 4
GATE_TAU = 16.0
LN_EPS = 1e-5
RMS_EPS = 1e-6
LANE = 128
VMEM_LIMIT = 60 * 1024 * 1024

PROMPT_MIXER = dict(nb=1, tc=256, chunk=64)
PROMPT_FFN = dict(nb=1, tc=1024, tf=512, sub=512)
MOD_TN = 1536
SAMPLE_MIXER_NB = 4
SAMPLE_FFN_TF = 512
GATE_LHS_TERM = (0, 1, 2, 0, 1, 0)
GATE_RHS_TERM = (0, 0, 0, 1, 1, 2)


def _bf16_terms(a):
    t1 = a.astype(BF16)
    r1 = a - t1.astype(F32)
    t2 = r1.astype(BF16)
    t3 = (r1 - t2.astype(F32)).astype(BF16)
    return t1, t2, t3


def _layernorm(y, g, b):
    mu = jnp.mean(y, axis=-1, keepdims=True)
    d = y - mu
    var = jnp.mean(d * d, axis=-1, keepdims=True)
    return d * lax.rsqrt(var + LN_EPS) * g + b


def _silu(a):
    return a * jax.nn.sigmoid(a)


def _mod_kernel(c_ref, w_ref, b_ref, o_ref):
    a = _silu(c_ref[...]).astype(BF16)
    o_ref[...] = jnp.dot(a, w_ref[...].astype(BF16), preferred_element_type=F32) + b_ref[...]


def _modulation(c_all, w_ada, b_ada, *, tn):
    m, d = c_all.shape
    n = w_ada.shape[1]
    return pl.pallas_call(
        _mod_kernel,
        grid=(n // tn,),
        in_specs=[pl.BlockSpec((m, d), lambda j: (0, 0)),
                  pl.BlockSpec((d, tn), lambda j: (0, j)),
                  pl.BlockSpec((1, tn), lambda j: (0, j))],
        out_specs=pl.BlockSpec((m, tn), lambda j: (0, j)),
        out_shape=jax.ShapeDtypeStruct((m, n), F32),
        compiler_params=pltpu.CompilerParams(dimension_semantics=("arbitrary",),
                                             vmem_limit_bytes=VMEM_LIMIT),
        name="adaln_modulation",
    )(c_all, w_ada, b_ada.reshape(1, n))


def _mixer_kernel(x_ref, mod_ref, conv0_ref, s0_ref, w_in_ref, w_fl_ref, w_conv_ref, w_f_ref, b_f_ref,
                  gain_ref, w_o_ref, ln_g_ref, ln_b_ref,
                  x1_ref, conv_ref, s_ref,
                  h_s, hb_s, u_s, qe_s, ke_s, kd_s, v_s, go_s, st_s, a_s, kv_s, stb_s, mix_s,
                  *, nb, tc, chunk, gate_rank, alpha):
    d_model = x_ref.shape[-1]
    d_conv = conv_ref.shape[-1]
    head_k, head_v = s_ref.shape[-2], s_ref.shape[-1]
    gla_dk = GLA_HEADS * head_k
    d_gla = GLA_HEADS * head_v
    rows = nb * tc
    n_chunks = tc // chunk
    tot_chunks = rows // chunk
    contract0 = (((0,), (0,)), ((), ()))
    contract1 = (((1,), (1,)), ((), ()))

    @pl.when(pl.program_id(1) == 0)
    def _():
        conv_ref[...] = conv0_ref[...]
        for i in range(nb):
            for hd in range(GLA_HEADS):
                st_s[i, hd] = s0_ref[i, hd].T

    h = x_ref[...] * (1.0 + mod_ref[:, 1]) + mod_ref[:, 0]
    h_s[...] = h.reshape(rows, d_model).astype(BF16)

    def proj(lo, width):
        return jnp.dot(h_s[...], w_in_ref[:, lo:lo + width], preferred_element_type=F32)

    o_hb, o_hc, o_hin = 0, d_conv, 2 * d_conv
    o_q = 3 * d_conv
    o_k = o_q + gla_dk
    o_v = o_k + gla_dk
    o_go = o_v + d_gla

    flr = jnp.dot(h_s[...], w_fl_ref[...], preferred_element_type=F32)
    t1 = flr.astype(BF16).astype(F32)
    t2 = (flr - t1).astype(BF16).astype(F32)
    t3 = (flr - t1 - t2).astype(BF16).astype(F32)
    terms = (t1, t2, t3)
    grp = lax.broadcasted_iota(jnp.int32, flr.shape, 1) // gate_rank
    lhs = terms[0]
    for g, term in enumerate(GATE_LHS_TERM):
        if term:
            lhs = jnp.where(grp == g, terms[term], lhs)
    z = jnp.dot(lhs.astype(BF16), w_f_ref[...], preferred_element_type=F32) + b_f_ref[...]
    q = proj(o_q, gla_dk) * (head_k ** -0.5)
    k = proj(o_k, gla_dk)
    lf = jax.nn.log_sigmoid(z) * (1.0 / GATE_TAU)
    lf1 = lf.astype(BF16)
    rem = lf - lf1.astype(F32)
    lf2 = rem.astype(BF16)
    lf3 = (rem - lf2.astype(F32)).astype(BF16)
    ri = lax.broadcasted_iota(jnp.int32, (rows, rows), 0)
    ci = lax.broadcasted_iota(jnp.int32, (rows, rows), 1)
    ltri = ((ri // chunk == ci // chunk) & (ci <= ri)).astype(BF16)
    parts = jnp.dot(ltri, jnp.concatenate([lf1, lf2, lf3], axis=1), preferred_element_type=F32)
    v_s[...] = proj(o_v, d_gla).astype(BF16)
    b = parts[:, 0:gla_dk] + parts[:, gla_dk:2 * gla_dk] + parts[:, 2 * gla_dk:3 * gla_dk]

    def chunk_rows(r):
        picked = [b[n * chunk + r:n * chunk + r + 1, :] for n in range(tot_chunks)]
        spread = jnp.concatenate([jnp.broadcast_to(e, (chunk, gla_dk)) for e in picked], axis=0)
        return picked, spread

    b_end, b_tot = chunk_rows(chunk - 1)
    chunk_decay = [jnp.exp(e) for e in b_end]
    b_mid, b_ref = chunk_rows(chunk // 2)
    mid_decay = [jnp.exp(e) for e in b_mid]

    qe_s[...] = (q * jnp.exp(b - b_ref)).astype(BF16)
    ke_s[...] = (k * jnp.exp(b_ref - b)).astype(BF16)
    kd_s[...] = (k * jnp.exp(b_tot - b)).astype(BF16)
    go_s[...] = proj(o_go, d_gla)

    cr = lax.broadcasted_iota(jnp.int32, (chunk, chunk), 0)
    cc = lax.broadcasted_iota(jnp.int32, (chunk, chunk), 1)
    causal = cr >= cc
    gain = gain_ref[...]

    head_chunks = [(n, hd) for n in range(tot_chunks) for hd in range(GLA_HEADS)]

    def slices(n, hd):
        return (slice(n * chunk, (n + 1) * chunk), slice(hd * head_k, (hd + 1) * head_k),
                slice(hd * head_v, (hd + 1) * head_v))

    def gla_scores_and_updates():
        for n, hd in head_chunks:
            rs, ks, vs = slices(n, hd)
            a = lax.dot_general(qe_s[rs, ks], ke_s[rs, ks], contract1, preferred_element_type=F32)
            a_s[rs, hd * chunk:(hd + 1) * chunk] = jnp.where(causal, a, 0.0).astype(BF16)
            kv_s[n, hd] = lax.dot_general(v_s[rs, vs], kd_s[rs, ks], contract0,
                                          preferred_element_type=F32)

    def gla_state_recurrence():
        for n, hd in head_chunks:
            _, ks, _ = slices(n, hd)
            st = st_s[n // n_chunks, hd]
            stb_s[n, hd] = (st * mid_decay[n][:, ks]).astype(BF16)
            st_s[n // n_chunks, hd] = st * chunk_decay[n][:, ks] + kv_s[n, hd]

    def gla_outputs():
        for n, hd in head_chunks:
            rs, ks, vs = slices(n, hd)
            o = (jnp.dot(a_s[rs, hd * chunk:(hd + 1) * chunk], v_s[rs, vs],
                         preferred_element_type=F32)
                 + lax.dot_general(qe_s[rs, ks], stb_s[n, hd], contract1,
                                   preferred_element_type=F32))
            o = o * lax.rsqrt(jnp.mean(o * o, axis=-1, keepdims=True) + RMS_EPS) * gain
            mix_s[rs, d_conv + hd * head_v:d_conv + (hd + 1) * head_v] = (
                o * _silu(go_s[rs, vs])).astype(BF16)

    row = lax.broadcasted_iota(jnp.int32, (tc, d_conv), 0)
    w0, w1, w2 = w_conv_ref[0:1, :], w_conv_ref[1:2, :], w_conv_ref[2:3, :]

    def conv_segment(i):
        rs = slice(i * tc, (i + 1) * tc)
        u = u_s[rs, :]
        st = conv_ref[i]
        p0, p1 = st[0:1, :], st[1:2, :]
        u1 = jnp.where(row == 0, p1, pltpu.roll(u, 1, axis=0))
        u2 = jnp.where(row == 0, p0, jnp.where(row == 1, p1, pltpu.roll(u, 2, axis=0)))
        conv = u2 * w0 + u1 * w1 + u * w2
        mix_s[rs, 0:d_conv] = (hb_s[rs, :] * conv).astype(BF16)
        conv_ref[i] = u[tc - (CONV_WIDTH - 1):tc, :]

    gla_scores_and_updates()
    u_s[...] = proj(o_hc, d_conv) * proj(o_hin, d_conv)
    gla_state_recurrence()
    hb_s[...] = proj(o_hb, d_conv)
    for i in range(nb):
        conv_segment(i)
    mo = jnp.dot(mix_s[:, 0:d_conv], w_o_ref[0:d_conv, :], preferred_element_type=F32)
    gla_outputs()
    half = rows // 2
    for r in (0, half):
        mo_r = mo[r:r + half] + jnp.dot(mix_s[r:r + half, d_conv:d_model],
                                        w_o_ref[d_conv:d_model, :], preferred_element_type=F32)
        if nb == 1:
            y = alpha * x_ref[0, r:r + half, :] + mod_ref[0, 2] * mo_r
            x1_ref[0, r:r + half, :] = _layernorm(y, ln_g_ref[...], ln_b_ref[...])
        else:
            bs_ = slice(r // tc, (r + half) // tc)
            y = alpha * x_ref[bs_] + mod_ref[bs_, 2] * mo_r.reshape(-1, tc, d_model)
            x1_ref[bs_] = _layernorm(y, ln_g_ref[...], ln_b_ref[...])

    @pl.when(pl.program_id(1) == pl.num_programs(1) - 1)
    def _():
        for i in range(nb):
            for hd in range(GLA_HEADS):
                s_ref[i, hd] = st_s[i, hd].T


def _mixer(x, mod, conv0, s0, w_in, w_fl, w_conv, w_f, b_f, gain, w_o, ln_g, ln_b,
           *, nb, tc, chunk, gate_rank, alpha):
    bsz, t, d = x.shape
    d_conv = conv0.shape[-1]
    heads, head_k, head_v = s0.shape[1:]
    gla_dk, d_gla = heads * head_k, heads * head_v
    rows = nb * tc
    const2 = lambda bi, ti: (0, 0)
    resident = functools.partial(pl.BlockSpec, index_map=const2, pipeline_mode=pl.Buffered(1))
    kern = functools.partial(_mixer_kernel, nb=nb, tc=tc, chunk=chunk, gate_rank=gate_rank,
                             alpha=alpha)
    return pl.pallas_call(
        kern,
        grid=(bsz // nb, t // tc),
        in_specs=[
            pl.BlockSpec((nb, tc, d), lambda bi, ti: (bi, ti, 0)),
            pl.BlockSpec((nb, 6, 1, d), lambda bi, ti: (bi, 0, 0, 0)),
            pl.BlockSpec((nb, CONV_WIDTH - 1, d_conv), lambda bi, ti: (bi, 0, 0)),
            pl.BlockSpec((nb, heads, head_k, head_v), lambda bi, ti: (bi, 0, 0, 0)),
            resident(w_in.shape),
            resident(w_fl.shape),
            pl.BlockSpec(w_conv.shape, const2),
            pl.BlockSpec(w_f.shape, const2),
            pl.BlockSpec(b_f.shape, const2),
            pl.BlockSpec(gain.shape, const2),
            resident(w_o.shape),
            pl.BlockSpec(ln_g.shape, const2),
            pl.BlockSpec(ln_b.shape, const2),
        ],
        out_specs=[
            pl.BlockSpec((nb, tc, d), lambda bi, ti: (bi, ti, 0)),
            pl.BlockSpec((nb, CONV_WIDTH - 1, d_conv), lambda bi, ti: (bi, 0, 0)),
            pl.BlockSpec((nb, heads, head_k, head_v), lambda bi, ti: (bi, 0, 0, 0)),
        ],
        out_shape=[
            jax.ShapeDtypeStruct((bsz, t, d), F32),
            jax.ShapeDtypeStruct(conv0.shape, F32),
            jax.ShapeDtypeStruct(s0.shape, F32),
        ],
        scratch_shapes=[
            pltpu.VMEM((rows, d), BF16),
            pltpu.VMEM((rows, d_conv), F32),
            pltpu.VMEM((rows, d_conv), F32),
            pltpu.VMEM((rows, gla_dk), BF16),
            pltpu.VMEM((rows, gla_dk), BF16),
            pltpu.VMEM((rows, gla_dk), BF16),
            pltpu.VMEM((rows, d_gla), BF16),
            pltpu.VMEM((rows, d_gla), F32),
            pltpu.VMEM((nb, heads, head_v, head_k), F32),
            pltpu.VMEM((rows, heads * chunk), BF16),
            pltpu.VMEM((rows // chunk, heads, head_v, head_k), F32),
            pltpu.VMEM((rows // chunk, heads, head_v, head_k), BF16),
            pltpu.VMEM((rows, d), BF16),
        ],
        compiler_params=pltpu.CompilerParams(dimension_semantics=("arbitrary", "arbitrary"),
                                             vmem_limit_bytes=VMEM_LIMIT),
        name="mixer",
    )(x, mod, conv0, s0, w_in, w_fl, w_conv, w_f, b_f, gain, w_o, ln_g, ln_b)


def _ffn_kernel(x1_ref, mod_ref, wg_ref, wu_ref, wd_ref, ln_g_ref, ln_b_ref, o_ref,
                h2_s, *, nb, tc, sub, alpha):
    d_model = x1_ref.shape[-1]
    rows = nb * tc
    f = pl.program_id(1)

    last = pl.num_programs(1) - 1
    if nb == 1:
        subs = [(slice(0, 1), slice(r, r + sub), slice(r, r + sub)) for r in range(0, tc, sub)]
    else:
        subs = [(slice(0, nb), slice(0, tc), slice(0, rows))]

    def swiglu(rows_flat):
        h2 = h2_s[rows_flat, :]
        a = jnp.dot(h2, wg_ref[...], preferred_element_type=F32)
        b = jnp.dot(h2, wu_ref[...], preferred_element_type=F32)
        part = jnp.dot((_silu(a) * b).astype(BF16), wd_ref[...], preferred_element_type=F32)
        return part.reshape(-1, tc if nb > 1 else sub, d_model)

    @pl.when(f == 0)
    def _():
        for bs_, rs_, fs_ in subs:
            h2 = x1_ref[bs_, rs_, :] * (1.0 + mod_ref[bs_, 4]) + mod_ref[bs_, 3]
            h2_s[fs_, :] = h2.reshape(-1, d_model).astype(BF16)
            o_ref[bs_, rs_, :] = swiglu(fs_)

    @pl.when((f > 0) & (f < last))
    def _():
        for bs_, rs_, fs_ in subs:
            o_ref[bs_, rs_, :] += swiglu(fs_)

    @pl.when(f == last)
    def _():
        for bs_, rs_, fs_ in subs:
            acc = o_ref[bs_, rs_, :] + swiglu(fs_)
            y = alpha * x1_ref[bs_, rs_, :] + mod_ref[bs_, 5] * acc
            o_ref[bs_, rs_, :] = _layernorm(y, ln_g_ref[...], ln_b_ref[...])


def _ffn(x1, mod, w_gate, w_up, w_down, ln_g, ln_b, *, nb, tc, tf, sub, alpha):
    bsz, t, d = x1.shape
    d_ff = w_gate.shape[1]
    rows = nb * tc
    t_tiles = t // tc
    assert d_ff // tf >= 2, "the first and last d_ff steps are distinct code paths"
    kern = functools.partial(_ffn_kernel, nb=nb, tc=tc, sub=sub, alpha=alpha)
    x_map = lambda ri, fi: (ri // t_tiles, ri % t_tiles, 0)
    return pl.pallas_call(
        kern,
        grid=((bsz // nb) * t_tiles, d_ff // tf),
        in_specs=[
            pl.BlockSpec((nb, tc, d), x_map),
            pl.BlockSpec((nb, 6, 1, d), lambda ri, fi: (ri // t_tiles, 0, 0, 0)),
            pl.BlockSpec((d, tf), lambda ri, fi: (0, fi)),
            pl.BlockSpec((d, tf), lambda ri, fi: (0, fi)),
            pl.BlockSpec((tf, d), lambda ri, fi: (fi, 0)),
            pl.BlockSpec(ln_g.shape, lambda ri, fi: (0, 0)),
            pl.BlockSpec(ln_b.shape, lambda ri, fi: (0, 0)),
        ],
        out_specs=pl.BlockSpec((nb, tc, d), x_map),
        out_shape=jax.ShapeDtypeStruct((bsz, t, d), F32),
        scratch_shapes=[pltpu.VMEM((rows, d), BF16)],
        compiler_params=pltpu.CompilerParams(dimension_semantics=("arbitrary", "arbitrary"),
                                             vmem_limit_bytes=VMEM_LIMIT),
        name="ffn",
    )(x1, mod, w_gate, w_up, w_down, ln_g, ln_b)


def kernel(x_prompt, x_sample, c_prompt, c_sample, cache_conv, state_gla, w_ada, b_ada, w_in,
           w_conv, w_f, b_f, gla_gain, w_o, ln1_g, ln1_b, w_gate, w_up, w_down, ln2_g, ln2_b):
    depth = w_ada.shape[0]
    bp, _, d_model = x_prompt.shape
    bs, t_s, _ = x_sample.shape
    d_conv = cache_conv.shape[-1]
    heads, head_k, head_v = state_gla.shape[2:]
    gate_rank = w_f.shape[1]
    alpha = (2.0 * depth) ** 0.25

    yp, ys = x_prompt, x_sample
    conv_p, gla_p, conv_s, gla_s = [], [], [], []
    for l in range(depth):
        c_all = jnp.concatenate([c_prompt, c_sample], axis=0)
        pad_rows = -c_all.shape[0] % 8
        c_all = jnp.pad(c_all, ((0, pad_rows), (0, 0)))
        mod = _modulation(c_all, w_ada[l], b_ada[l], tn=MOD_TN)
        mod_p = mod[:bp].reshape(bp, 6, 1, d_model)
        mod_s = mod[bp:bp + bs].reshape(bs, 6, 1, d_model)

        n_main = w_in.shape[-1] - gate_rank
        copies = len(GATE_LHS_TERM)
        fill = LANE - copies * gate_rank
        w_in_b = w_in[l].astype(BF16)
        w_fl_b = jnp.concatenate([w_in[l][:, n_main:]] * copies
                                 + [jnp.zeros((d_model, fill), F32)], axis=1).astype(BF16)
        wf_terms = _bf16_terms(w_f[l])
        w_f_p = jnp.concatenate([wf_terms[t] for t in GATE_RHS_TERM]
                                + [jnp.zeros((fill, w_f.shape[-1]), BF16)], axis=0)
        w_o_b = w_o[l].astype(BF16)
        wg_b, wu_b, wd_b = w_gate[l].astype(BF16), w_up[l].astype(BF16), w_down[l].astype(BF16)
        row = lambda a: a.reshape(1, -1)
        mixer_w = (w_in_b, w_fl_b, w_conv[l], w_f_p, row(b_f[l]), row(gla_gain[l]), w_o_b,
                   row(ln1_g[l]), row(ln1_b[l]))
        ffn_w = (wg_b, wu_b, wd_b, row(ln2_g[l]), row(ln2_b[l]))

        conv0 = jnp.zeros((bp, CONV_WIDTH - 1, d_conv), F32)
        gla0 = jnp.zeros((bp, heads, head_k, head_v), F32)
        x1p, cp, sp = _mixer(yp, mod_p, conv0, gla0, *mixer_w, gate_rank=gate_rank, alpha=alpha,
                             **PROMPT_MIXER)
        yp = _ffn(x1p, mod_p, *ffn_w, alpha=alpha, **PROMPT_FFN)

        x1s, cs, ss = _mixer(ys, mod_s, cache_conv[l], state_gla[l], *mixer_w,
                             nb=SAMPLE_MIXER_NB, tc=t_s, chunk=t_s, gate_rank=gate_rank,
                             alpha=alpha)
        ys = _ffn(x1s, mod_s, *ffn_w, nb=bs, tc=t_s, tf=SAMPLE_FFN_TF, sub=t_s, alpha=alpha)

        conv_p.append(cp)
        gla_p.append(sp)
        conv_s.append(cs)
        gla_s.append(ss)
    return (yp, ys, jnp.stack(conv_p), jnp.stack(gla_p), jnp.stack(conv_s), jnp.stack(gla_s))
```

```python
import functools

import jax
import jax.numpy as jnp
from jax import lax
from jax.experimental import pallas as pl
from jax.experimental.pallas import tpu as pltpu

F32 = jnp.float32
BF16 = jnp.bfloat16

CONV_WIDTH = 3
GLA_HEADS = 4
GATE_TAU = 16.0
LN_EPS = 1e-5
RMS_EPS = 1e-6
LANE = 128
VMEM_LIMIT = 60 * 1024 * 1024

PROMPT_MIXER = dict(nb=1, tc=256, chunk=64)
PROMPT_FFN = dict(nb=1, tc=1024, tf=512, sub=512)
MOD_TN = 1536
SAMPLE_MIXER_NB = 4
SAMPLE_FFN_TF = 512
GATE_LHS_TERM = (0, 1, 2, 0, 1, 0)
GATE_RHS_TERM = (0, 0, 0, 1, 1, 2)


def _bf16_terms(a):
    t1 = a.astype(BF16)
    r1 = a - t1.astype(F32)
    t2 = r1.astype(BF16)
    t3 = (r1 - t2.astype(F32)).astype(BF16)
    return t1, t2, t3


def _layernorm(y, g, b):
    mu = jnp.mean(y, axis=-1, keepdims=True)
    d = y - mu
    var = jnp.mean(d * d, axis=-1, keepdims=True)
    return d * lax.rsqrt(var + LN_EPS) * g + b


def _silu(a):
    return a * jax.nn.sigmoid(a)


def _mod_kernel(c_ref, w_ref, b_ref, o_ref):
    a = _silu(c_ref[...]).astype(BF16)
    o_ref[...] = jnp.dot(a, w_ref[...].astype(BF16), preferred_element_type=F32) + b_ref[...]


def _modulation(c_all, w_ada, b_ada, *, tn):
    m, d = c_all.shape
    n = w_ada.shape[1]
    return pl.pallas_call(
        _mod_kernel,
        grid=(n // tn,),
        in_specs=[pl.BlockSpec((m, d), lambda j: (0, 0)),
                  pl.BlockSpec((d, tn), lambda j: (0, j)),
                  pl.BlockSpec((1, tn), lambda j: (0, j))],
        out_specs=pl.BlockSpec((m, tn), lambda j: (0, j)),
        out_shape=jax.ShapeDtypeStruct((m, n), F32),
        compiler_params=pltpu.CompilerParams(dimension_semantics=("arbitrary",),
                                             vmem_limit_bytes=VMEM_LIMIT),
        name="adaln_modulation",
    )(c_all, w_ada, b_ada.reshape(1, n))


def _mixer_kernel(x_ref, mod_ref, conv0_ref, s0_ref, w_in_ref, w_fl_ref, w_conv_ref, w_f_ref, b_f_ref,
                  gain_ref, w_o_ref, ln_g_ref, ln_b_ref,
                  x1_ref, conv_ref, s_ref,
                  h_s, hb_s, u_s, qe_s, ke_s, kd_s, v_s, go_s, st_s, a_s, kv_s, stb_s, mix_s,
                  *, nb, tc, chunk, gate_rank, alpha):
    d_model = x_ref.shape[-1]
    d_conv = conv_ref.shape[-1]
    head_k, head_v = s_ref.shape[-2], s_ref.shape[-1]
    gla_dk = GLA_HEADS * head_k
    d_gla = GLA_HEADS * head_v
    rows = nb * tc
    n_chunks = tc // chunk
    tot_chunks = rows // chunk
    contract0 = (((0,), (0,)), ((), ()))
    contract1 = (((1,), (1,)), ((), ()))

    @pl.when(pl.program_id(1) == 0)
    def _():
        conv_ref[...] = conv0_ref[...]
        for i in range(nb):
            for hd in range(GLA_HEADS):
                st_s[i, hd] = s0_ref[i, hd].T

    h = x_ref[...] * (1.0 + mod_ref[:, 1]) + mod_ref[:, 0]
    h_s[...] = h.reshape(rows, d_model).astype(BF16)

    def proj(lo, width):
        return jnp.dot(h_s[...], w_in_ref[:, lo:lo + width], preferred_element_type=F32)

    o_hb, o_hc, o_hin = 0, d_conv, 2 * d_conv
    o_q = 3 * d_conv
    o_k = o_q + gla_dk
    o_v = o_k + gla_dk
    o_go = o_v + d_gla

    flr = jnp.dot(h_s[...], w_fl_ref[...], preferred_element_type=F32)
    t1 = flr.astype(BF16).astype(F32)
    t2 = (flr - t1).astype(BF16).astype(F32)
    t3 = (flr - t1 - t2).astype(BF16).astype(F32)
    terms = (t1, t2, t3)
    grp = lax.broadcasted_iota(jnp.int32, flr.shape, 1) // gate_rank
    lhs = terms[0]
    for g, term in enumerate(GATE_LHS_TERM):
        if term:
            lhs = jnp.where(grp == g, terms[term], lhs)
    z = jnp.dot(lhs.astype(BF16), w_f_ref[...], preferred_element_type=F32) + b_f_ref[...]
    q = proj(o_q, gla_dk) * (head_k ** -0.5)
    k = proj(o_k, gla_dk)
    lf = jax.nn.log_sigmoid(z) * (1.0 / GATE_TAU)
    lf1 = lf.astype(BF16)
    rem = lf - lf1.astype(F32)
    lf2 = rem.astype(BF16)
    lf3 = (rem - lf2.astype(F32)).astype(BF16)
    ri = lax.broadcasted_iota(jnp.int32, (rows, rows), 0)
    ci = lax.broadcasted_iota(jnp.int32, (rows, rows), 1)
    ltri = ((ri // chunk == ci // chunk) & (ci <= ri)).astype(BF16)
    parts = jnp.dot(ltri, jnp.concatenate([lf1, lf2, lf3], axis=1), preferred_element_type=F32)
    v_s[...] = proj(o_v, d_gla).astype(BF16)
    b = parts[:, 0:gla_dk] + parts[:, gla_dk:2 * gla_dk] + parts[:, 2 * gla_dk:3 * gla_dk]

    def chunk_rows(r):
        picked = [b[n * chunk + r:n * chunk + r + 1, :] for n in range(tot_chunks)]
        spread = jnp.concatenate([jnp.broadcast_to(e, (chunk, gla_dk)) for e in picked], axis=0)
        return picked, spread

    b_end, b_tot = chunk_rows(chunk - 1)
    chunk_decay = [jnp.exp(e) for e in b_end]
    b_mid, b_ref = chunk_rows(chunk // 2)
    mid_decay = [jnp.exp(e) for e in b_mid]

    qe_s[...] = (q * jnp.exp(b - b_ref)).astype(BF16)
    ke_s[...] = (k * jnp.exp(b_ref - b)).astype(BF16)
    kd_s[...] = (k * jnp.exp(b_tot - b)).astype(BF16)
    go_s[...] = proj(o_go, d_gla)

    cr = lax.broadcasted_iota(jnp.int32, (chunk, chunk), 0)
    cc = lax.broadcasted_iota(jnp.int32, (chunk, chunk), 1)
    causal = cr >= cc
    gain = gain_ref[...]

    head_chunks = [(n, hd) for n in range(tot_chunks) for hd in range(GLA_HEADS)]

    def slices(n, hd):
        return (slice(n * chunk, (n + 1) * chunk), slice(hd * head_k, (hd + 1) * head_k),
                slice(hd * head_v, (hd + 1) * head_v))

    def gla_scores_and_updates():
        for n, hd in head_chunks:
            rs, ks, vs = slices(n, hd)
            a = lax.dot_general(qe_s[rs, ks], ke_s[rs, ks], contract1, preferred_element_type=F32)
            a_s[rs, hd * chunk:(hd + 1) * chunk] = jnp.where(causal, a, 0.0).astype(BF16)
            kv_s[n, hd] = lax.dot_general(v_s[rs, vs], kd_s[rs, ks], contract0,
                                          preferred_element_type=F32)

    def gla_state_recurrence():
        for n, hd in head_chunks:
            _, ks, _ = slices(n, hd)
            st = st_s[n // n_chunks, hd]
            stb_s[n, hd] = (st * mid_decay[n][:, ks]).astype(BF16)
            st_s[n // n_chunks, hd] = st * chunk_decay[n][:, ks] + kv_s[n, hd]

    def gla_outputs():
        for n, hd in head_chunks:
            rs, ks, vs = slices(n, hd)
            o = (jnp.dot(a_s[rs, hd * chunk:(hd + 1) * chunk], v_s[rs, vs],
                         preferred_element_type=F32)
                 + lax.dot_general(qe_s[rs, ks], stb_s[n, hd], contract1,
                                   preferred_element_type=F32))
            o = o * lax.rsqrt(jnp.mean(o * o, axis=-1, keepdims=True) + RMS_EPS) * gain
            mix_s[rs, d_conv + hd * head_v:d_conv + (hd + 1) * head_v] = (
                o * _silu(go_s[rs, vs])).astype(BF16)

    row = lax.broadcasted_iota(jnp.int32, (tc, d_conv), 0)
    w0, w1, w2 = w_conv_ref[0:1, :], w_conv_ref[1:2, :], w_conv_ref[2:3, :]

    def conv_segment(i):
        rs = slice(i * tc, (i + 1) * tc)
        u = u_s[rs, :]
        st = conv_ref[i]
        p0, p1 = st[0:1, :], st[1:2, :]
        u1 = jnp.where(row == 0, p1, pltpu.roll(u, 1, axis=0))
        u2 = jnp.where(row == 0, p0, jnp.where(row == 1, p1, pltpu.roll(u, 2, axis=0)))
        conv = u2 * w0 + u1 * w1 + u * w2
        mix_s[rs, 0:d_conv] = (hb_s[rs, :] * conv).astype(BF16)
        conv_ref[i] = u[tc - (CONV_WIDTH - 1):tc, :]

    gla_scores_and_updates()
    u_s[...] = proj(o_hc, d_conv) * proj(o_hin, d_conv)
    gla_state_recurrence()
    hb_s[...] = proj(o_hb, d_conv)
    for i in range(nb):
        conv_segment(i)
    mo = jnp.dot(mix_s[:, 0:d_conv], w_o_ref[0:d_conv, :], preferred_element_type=F32)
    gla_outputs()
    half = rows // 2
    for r in (0, half):
        mo_r = mo[r:r + half] + jnp.dot(mix_s[r:r + half, d_conv:d_model],
                                        w_o_ref[d_conv:d_model, :], preferred_element_type=F32)
        if nb == 1:
            y = alpha * x_ref[0, r:r + half, :] + mod_ref[0, 2] * mo_r
            x1_ref[0, r:r + half, :] = _layernorm(y, ln_g_ref[...], ln_b_ref[...])
        else:
            bs_ = slice(r // tc, (r + half) // tc)
            y = alpha * x_ref[bs_] + mod_ref[bs_, 2] * mo_r.reshape(-1, tc, d_model)
            x1_ref[bs_] = _layernorm(y, ln_g_ref[...], ln_b_ref[...])

    @pl.when(pl.program_id(1) == pl.num_programs(1) - 1)
    def _():
        for i in range(nb):
            for hd in range(GLA_HEADS):
                s_ref[i, hd] = st_s[i, hd].T


def _mixer(x, mod, conv0, s0, w_in, w_fl, w_conv, w_f, b_f, gain, w_o, ln_g, ln_b,
           *, nb, tc, chunk, gate_rank, alpha):
    bsz, t, d = x.shape
    d_conv = conv0.shape[-1]
    heads, head_k, head_v = s0.shape[1:]
    gla_dk, d_gla = heads * head_k, heads * head_v
    rows = nb * tc
    const2 = lambda bi, ti: (0, 0)
    resident = functools.partial(pl.BlockSpec, index_map=const2, pipeline_mode=pl.Buffered(1))
    kern = functools.partial(_mixer_kernel, nb=nb, tc=tc, chunk=chunk, gate_rank=gate_rank,
                             alpha=alpha)
    return pl.pallas_call(
        kern,
        grid=(bsz // nb, t // tc),
        in_specs=[
            pl.BlockSpec((nb, tc, d), lambda bi, ti: (bi, ti, 0)),
            pl.BlockSpec((nb, 6, 1, d), lambda bi, ti: (bi, 0, 0, 0)),
            pl.BlockSpec((nb, CONV_WIDTH - 1, d_conv), lambda bi, ti: (bi, 0, 0)),
            pl.BlockSpec((nb, heads, head_k, head_v), lambda bi, ti: (bi, 0, 0, 0)),
            resident(w_in.shape),
            resident(w_fl.shape),
            pl.BlockSpec(w_conv.shape, const2),
            pl.BlockSpec(w_f.shape, const2),
            pl.BlockSpec(b_f.shape, const2),
            pl.BlockSpec(gain.shape, const2),
            resident(w_o.shape),
            pl.BlockSpec(ln_g.shape, const2),
            pl.BlockSpec(ln_b.shape, const2),
        ],
        out_specs=[
            pl.BlockSpec((nb, tc, d), lambda bi, ti: (bi, ti, 0)),
            pl.BlockSpec((nb, CONV_WIDTH - 1, d_conv), lambda bi, ti: (bi, 0, 0)),
            pl.BlockSpec((nb, heads, head_k, head_v), lambda bi, ti: (bi, 0, 0, 0)),
        ],
        out_shape=[
            jax.ShapeDtypeStruct((bsz, t, d), F32),
            jax.ShapeDtypeStruct(conv0.shape, F32),
            jax.ShapeDtypeStruct(s0.shape, F32),
        ],
        scratch_shapes=[
            pltpu.VMEM((rows, d), BF16),
            pltpu.VMEM((rows, d_conv), F32),
            pltpu.VMEM((rows, d_conv), F32),
            pltpu.VMEM((rows, gla_dk), BF16),
            pltpu.VMEM((rows, gla_dk), BF16),
            pltpu.VMEM((rows, gla_dk), BF16),
            pltpu.VMEM((rows, d_gla), BF16),
            pltpu.VMEM((rows, d_gla), F32),
            pltpu.VMEM((nb, heads, head_v, head_k), F32),
            pltpu.VMEM((rows, heads * chunk), BF16),
            pltpu.VMEM((rows // chunk, heads, head_v, head_k), F32),
            pltpu.VMEM((rows // chunk, heads, head_v, head_k), BF16),
            pltpu.VMEM((rows, d), BF16),
        ],
        compiler_params=pltpu.CompilerParams(dimension_semantics=("arbitrary", "arbitrary"),
                                             vmem_limit_bytes=VMEM_LIMIT),
        name="mixer",
    )(x, mod, conv0, s0, w_in, w_fl, w_conv, w_f, b_f, gain, w_o, ln_g, ln_b)


def _ffn_kernel(x1_ref, mod_ref, wg_hbm, wu_hbm, wd_hbm, ln_g_ref, ln_b_ref, o_ref,
                h2_s, wg_s, wu_s, wd_s, sems, *, nb, tc, tf, sub, alpha):
    d_model = x1_ref.shape[-1]
    n_f = wg_hbm.shape[1] // tf
    rows = nb * tc
    r = pl.program_id(0)
    first_slot = lax.rem(r * n_f, 2)

    def slot_of(f):
        return lax.rem(first_slot + f, 2)

    def weight_copies(f, slot):
        cols = pl.ds(f * tf if isinstance(f, int) else pl.multiple_of(f * tf, tf), tf)
        return (pltpu.make_async_copy(wg_hbm.at[:, cols], wg_s.at[slot], sems.at[0, slot]),
                pltpu.make_async_copy(wu_hbm.at[:, cols], wu_s.at[slot], sems.at[1, slot]),
                pltpu.make_async_copy(wd_hbm.at[cols, :], wd_s.at[slot], sems.at[2, slot]))

    def start(f, slot):
        for c in weight_copies(f, slot):
            c.start()

    def wait(f, slot):
        for c in weight_copies(f, slot):
            c.wait()

    if nb == 1:
        subs = [(slice(0, 1), slice(q, q + sub), slice(q, q + sub)) for q in range(0, tc, sub)]
    else:
        subs = [(slice(0, nb), slice(0, tc), slice(0, rows))]

    def swiglu(rows_flat, slot):
        h2 = h2_s[rows_flat, :]
        a = jnp.dot(h2, wg_s[slot], preferred_element_type=F32)
        b = jnp.dot(h2, wu_s[slot], preferred_element_type=F32)
        part = jnp.dot((_silu(a) * b).astype(BF16), wd_s[slot], preferred_element_type=F32)
        return part.reshape(-1, tc if nb > 1 else sub, d_model)

    @pl.when(r == 0)
    def _():
        start(0, slot_of(0))

    start(1, slot_of(1))
    wait(0, slot_of(0))
    for bs_, rs_, fs_ in subs:
        h2 = x1_ref[bs_, rs_, :] * (1.0 + mod_ref[bs_, 4]) + mod_ref[bs_, 3]
        h2_s[fs_, :] = h2.reshape(-1, d_model).astype(BF16)
        o_ref[bs_, rs_, :] = swiglu(fs_, slot_of(0))

    def middle(f, carry):
        start(f + 1, slot_of(f + 1))
        wait(f, slot_of(f))
        for bs_, rs_, fs_ in subs:
            o_ref[bs_, rs_, :] += swiglu(fs_, slot_of(f))
        return carry

    lax.fori_loop(1, n_f - 1, middle, 0)

    @pl.when(r + 1 < pl.num_programs(0))
    def _():
        start(0, slot_of(n_f))

    wait(n_f - 1, slot_of(n_f - 1))
    for bs_, rs_, fs_ in subs:
        acc = o_ref[bs_, rs_, :] + swiglu(fs_, slot_of(n_f - 1))
        y = alpha * x1_ref[bs_, rs_, :] + mod_ref[bs_, 5] * acc
        o_ref[bs_, rs_, :] = _layernorm(y, ln_g_ref[...], ln_b_ref[...])


def _ffn(x1, mod, w_gate, w_up, w_down, ln_g, ln_b, *, nb, tc, tf, sub, alpha):
    bsz, t, d = x1.shape
    d_ff = w_gate.shape[1]
    rows = nb * tc
    t_tiles = t // tc
    assert d_ff % tf == 0 and d_ff // tf >= 3, "first, middle and last d_ff tiles are peeled"
    kern = functools.partial(_ffn_kernel, nb=nb, tc=tc, tf=tf, sub=sub, alpha=alpha)
    x_map = lambda ri: (ri // t_tiles, ri % t_tiles, 0)
    hbm = pl.BlockSpec(memory_space=pl.ANY)
    return pl.pallas_call(
        kern,
        grid=((bsz // nb) * t_tiles,),
        in_specs=[
            pl.BlockSpec((nb, tc, d), x_map),
            pl.BlockSpec((nb, 6, 1, d), lambda ri: (ri // t_tiles, 0, 0, 0)),
            hbm, hbm, hbm,
            pl.BlockSpec(ln_g.shape, lambda ri: (0, 0)),
            pl.BlockSpec(ln_b.shape, lambda ri: (0, 0)),
        ],
        out_specs=pl.BlockSpec((nb, tc, d), x_map),
        out_shape=jax.ShapeDtypeStruct((bsz, t, d), F32),
        scratch_shapes=[
            pltpu.VMEM((rows, d), BF16),
            pltpu.VMEM((2, d, tf), BF16),
            pltpu.VMEM((2, d, tf), BF16),
            pltpu.VMEM((2, tf, d), BF16),
            pltpu.SemaphoreType.DMA((3, 2)),
        ],
        compiler_params=pltpu.CompilerParams(dimension_semantics=("arbitrary",),
                                             vmem_limit_bytes=VMEM_LIMIT),
        name="ffn",
    )(x1, mod, w_gate, w_up, w_down, ln_g, ln_b)


def kernel(x_prompt, x_sample, c_prompt, c_sample, cache_conv, state_gla, w_ada, b_ada, w_in,
           w_conv, w_f, b_f, gla_gain, w_o, ln1_g, ln1_b, w_gate, w_up, w_down, ln2_g, ln2_b):
    depth = w_ada.shape[0]
    bp, _, d_model = x_prompt.shape
    bs, t_s, _ = x_sample.shape
    d_conv = cache_conv.shape[-1]
    heads, head_k, head_v = state_gla.shape[2:]
    gate_rank = w_f.shape[1]
    alpha = (2.0 * depth) ** 0.25

    yp, ys = x_prompt, x_sample
    conv_p, gla_p, conv_s, gla_s = [], [], [], []
    for l in range(depth):
        c_all = jnp.concatenate([c_prompt, c_sample], axis=0)
        pad_rows = -c_all.shape[0] % 8
        c_all = jnp.pad(c_all, ((0, pad_rows), (0, 0)))
        mod = _modulation(c_all, w_ada[l], b_ada[l], tn=MOD_TN)
        mod_p = mod[:bp].reshape(bp, 6, 1, d_model)
        mod_s = mod[bp:bp + bs].reshape(bs, 6, 1, d_model)

        n_main = w_in.shape[-1] - gate_rank
        copies = len(GATE_LHS_TERM)
        fill = LANE - copies * gate_rank
        w_in_b = w_in[l].astype(BF16)
        w_fl_b = jnp.concatenate([w_in[l][:, n_main:]] * copies
                                 + [jnp.zeros((d_model, fill), F32)], axis=1).astype(BF16)
        wf_terms = _bf16_terms(w_f[l])
        w_f_p = jnp.concatenate([wf_terms[t] for t in GATE_RHS_TERM]
                                + [jnp.zeros((fill, w_f.shape[-1]), BF16)], axis=0)
        w_o_b = w_o[l].astype(BF16)
        wg_b, wu_b, wd_b = w_gate[l].astype(BF16), w_up[l].astype(BF16), w_down[l].astype(BF16)
        row = lambda a: a.reshape(1, -1)
        mixer_w = (w_in_b, w_fl_b, w_conv[l], w_f_p, row(b_f[l]), row(gla_gain[l]), w_o_b,
                   row(ln1_g[l]), row(ln1_b[l]))
        ffn_w = (wg_b, wu_b, wd_b, row(ln2_g[l]), row(ln2_b[l]))

        conv0 = jnp.zeros((bp, CONV_WIDTH - 1, d_conv), F32)
        gla0 = jnp.zeros((bp, heads, head_k, head_v), F32)
        x1p, cp, sp = _mixer(yp, mod_p, conv0, gla0, *mixer_w, gate_rank=gate_rank, alpha=alpha,
                             **PROMPT_MIXER)
        yp = _ffn(x1p, mod_p, *ffn_w, alpha=alpha, **PROMPT_FFN)

        x1s, cs, ss = _mixer(ys, mod_s, cache_conv[l], state_gla[l], *mixer_w,
                             nb=SAMPLE_MIXER_NB, tc=t_s, chunk=t_s, gate_rank=gate_rank,
                             alpha=alpha)
        ys = _ffn(x1s, mod_s, *ffn_w, nb=bs, tc=t_s, tf=SAMPLE_FFN_TF, sub=t_s, alpha=alpha)

        conv_p.append(cp)
        gla_p.append(sp)
        conv_s.append(cs)
        gla_s.append(ss)
    return (yp, ys, jnp.stack(conv_p), jnp.stack(gla_p), jnp.stack(conv_s), jnp.stack(gla_s))
```

```python
import functools

import jax
import jax.numpy as jnp
from jax import lax
from jax.experimental import pallas as pl
from jax.experimental.pallas import tpu as pltpu

F32 = jnp.float32
BF16 = jnp.bfloat16

CONV_WIDTH = 3
GLA_HEADS = 4
GATE_TAU = 16.0
LN_EPS = 1e-5
RMS_EPS = 1e-6
LANE = 128
BF16_ROWS = 16
VMEM_LIMIT = 60 * 1024 * 1024

PROMPT_MIXER = dict(nb=1, tc=256, chunk=64)
PROMPT_FFN = dict(nb=1, tc=1024, tf=512, sub=512)
MOD_TN = 1536
SAMPLE_MIXER_NB = 4
SAMPLE_FFN_TF = 512
GATE_LHS_TERM = (0, 1, 2, 0, 1, 0)
GATE_RHS_TERM = (0, 0, 0, 1, 1, 2)


def _bf16_terms(a):
    t1 = a.astype(BF16)
    r1 = a - t1.astype(F32)
    t2 = r1.astype(BF16)
    t3 = (r1 - t2.astype(F32)).astype(BF16)
    return t1, t2, t3


def _layernorm(y, g, b):
    mu = jnp.mean(y, axis=-1, keepdims=True)
    d = y - mu
    var = jnp.mean(d * d, axis=-1, keepdims=True)
    return d * lax.rsqrt(var + LN_EPS) * g + b


def _silu(a):
    return a * jax.nn.sigmoid(a)


def _mod_kernel(c_ref, w_ref, b_ref, o_ref):
    a = _silu(c_ref[...]).astype(BF16)
    o_ref[...] = jnp.dot(a, w_ref[...].astype(BF16), preferred_element_type=F32) + b_ref[...]


def _modulation(c_all, w_ada, b_ada, *, tn):
    m, d = c_all.shape
    n = w_ada.shape[1]
    return pl.pallas_call(
        _mod_kernel,
        grid=(n // tn,),
        in_specs=[pl.BlockSpec((m, d), lambda j: (0, 0)),
                  pl.BlockSpec((d, tn), lambda j: (0, j)),
                  pl.BlockSpec((1, tn), lambda j: (0, j))],
        out_specs=pl.BlockSpec((m, tn), lambda j: (0, j)),
        out_shape=jax.ShapeDtypeStruct((m, n), F32),
        compiler_params=pltpu.CompilerParams(dimension_semantics=("arbitrary",),
                                             vmem_limit_bytes=VMEM_LIMIT),
        name="adaln_modulation",
    )(c_all, w_ada, b_ada.reshape(1, n))


def _mixer_kernel(x_ref, mod_ref, conv0_ref, s0_ref, w_in_ref, w_fl_ref, w_conv_ref, w_f_ref, b_f_ref,
                  gain_ref, w_o_ref, ln_g_ref, ln_b_ref,
                  x1_ref, conv_ref, s_ref,
                  h_s, hb_s, u_s, qe_s, ke_s, kd_s, v_s, go_s, st_s, a_s, kv_s, stb_s, mix_s,
                  *, nb, tc, chunk, gate_rank, alpha, side_work=None):
    d_model = x_ref.shape[-1]
    d_conv = conv_ref.shape[-1]
    head_k, head_v = s_ref.shape[-2], s_ref.shape[-1]
    gla_dk = GLA_HEADS * head_k
    d_gla = GLA_HEADS * head_v
    rows = nb * tc
    n_chunks = tc // chunk
    tot_chunks = rows // chunk
    contract0 = (((0,), (0,)), ((), ()))
    contract1 = (((1,), (1,)), ((), ()))

    @pl.when(pl.program_id(1) == 0)
    def _():
        conv_ref[...] = conv0_ref[...]
        for i in range(nb):
            for hd in range(GLA_HEADS):
                st_s[i, hd] = s0_ref[i, hd].T

    h = x_ref[...] * (1.0 + mod_ref[:, 1]) + mod_ref[:, 0]
    h_s[...] = h.reshape(rows, d_model).astype(BF16)

    def proj(lo, width):
        return jnp.dot(h_s[...], w_in_ref[:, lo:lo + width], preferred_element_type=F32)

    o_hb, o_hc, o_hin = 0, d_conv, 2 * d_conv
    o_q = 3 * d_conv
    o_k = o_q + gla_dk
    o_v = o_k + gla_dk
    o_go = o_v + d_gla

    flr = jnp.dot(h_s[...], w_fl_ref[...], preferred_element_type=F32)
    t1 = flr.astype(BF16).astype(F32)
    t2 = (flr - t1).astype(BF16).astype(F32)
    t3 = (flr - t1 - t2).astype(BF16).astype(F32)
    terms = (t1, t2, t3)
    grp = lax.broadcasted_iota(jnp.int32, flr.shape, 1) // gate_rank
    lhs = terms[0]
    for g, term in enumerate(GATE_LHS_TERM):
        if term:
            lhs = jnp.where(grp == g, terms[term], lhs)
    z = jnp.dot(lhs.astype(BF16), w_f_ref[...], preferred_element_type=F32) + b_f_ref[...]
    q = proj(o_q, gla_dk) * (head_k ** -0.5)
    k = proj(o_k, gla_dk)
    lf = jax.nn.log_sigmoid(z) * (1.0 / GATE_TAU)
    lf1 = lf.astype(BF16)
    rem = lf - lf1.astype(F32)
    lf2 = rem.astype(BF16)
    lf3 = (rem - lf2.astype(F32)).astype(BF16)
    ri = lax.broadcasted_iota(jnp.int32, (rows, rows), 0)
    ci = lax.broadcasted_iota(jnp.int32, (rows, rows), 1)
    ltri = ((ri // chunk == ci // chunk) & (ci <= ri)).astype(BF16)
    parts = jnp.dot(ltri, jnp.concatenate([lf1, lf2, lf3], axis=1), preferred_element_type=F32)
    v_s[...] = proj(o_v, d_gla).astype(BF16)
    b = parts[:, 0:gla_dk] + parts[:, gla_dk:2 * gla_dk] + parts[:, 2 * gla_dk:3 * gla_dk]

    def chunk_rows(r):
        picked = [b[n * chunk + r:n * chunk + r + 1, :] for n in range(tot_chunks)]
        spread = jnp.concatenate([jnp.broadcast_to(e, (chunk, gla_dk)) for e in picked], axis=0)
        return picked, spread

    b_end, b_tot = chunk_rows(chunk - 1)
    chunk_decay = [jnp.exp(e) for e in b_end]
    b_mid, b_ref = chunk_rows(chunk // 2)
    mid_decay = [jnp.exp(e) for e in b_mid]

    qe_s[...] = (q * jnp.exp(b - b_ref)).astype(BF16)
    ke_s[...] = (k * jnp.exp(b_ref - b)).astype(BF16)
    kd_s[...] = (k * jnp.exp(b_tot - b)).astype(BF16)
    go_s[...] = proj(o_go, d_gla)

    cr = lax.broadcasted_iota(jnp.int32, (chunk, chunk), 0)
    cc = lax.broadcasted_iota(jnp.int32, (chunk, chunk), 1)
    causal = cr >= cc
    gain = gain_ref[...]

    head_chunks = [(n, hd) for n in range(tot_chunks) for hd in range(GLA_HEADS)]

    def slices(n, hd):
        return (slice(n * chunk, (n + 1) * chunk), slice(hd * head_k, (hd + 1) * head_k),
                slice(hd * head_v, (hd + 1) * head_v))

    def gla_scores_and_updates():
        for n, hd in head_chunks:
            rs, ks, vs = slices(n, hd)
            a = lax.dot_general(qe_s[rs, ks], ke_s[rs, ks], contract1, preferred_element_type=F32)
            a_s[rs, hd * chunk:(hd + 1) * chunk] = jnp.where(causal, a, 0.0).astype(BF16)
            kv_s[n, hd] = lax.dot_general(v_s[rs, vs], kd_s[rs, ks], contract0,
                                          preferred_element_type=F32)

    def gla_state_recurrence():
        for n, hd in head_chunks:
            _, ks, _ = slices(n, hd)
            st = st_s[n // n_chunks, hd]
            stb_s[n, hd] = (st * mid_decay[n][:, ks]).astype(BF16)
            st_s[n // n_chunks, hd] = st * chunk_decay[n][:, ks] + kv_s[n, hd]

    def gla_outputs():
        for n, hd in head_chunks:
            rs, ks, vs = slices(n, hd)
            o = (jnp.dot(a_s[rs, hd * chunk:(hd + 1) * chunk], v_s[rs, vs],
                         preferred_element_type=F32)
                 + lax.dot_general(qe_s[rs, ks], stb_s[n, hd], contract1,
                                   preferred_element_type=F32))
            o = o * lax.rsqrt(jnp.mean(o * o, axis=-1, keepdims=True) + RMS_EPS) * gain
            mix_s[rs, d_conv + hd * head_v:d_conv + (hd + 1) * head_v] = (
                o * _silu(go_s[rs, vs])).astype(BF16)

    row = lax.broadcasted_iota(jnp.int32, (tc, d_conv), 0)
    w0, w1, w2 = w_conv_ref[0:1, :], w_conv_ref[1:2, :], w_conv_ref[2:3, :]

    def conv_segment(i):
        rs = slice(i * tc, (i + 1) * tc)
        u = u_s[rs, :]
        st = conv_ref[i]
        p0, p1 = st[0:1, :], st[1:2, :]
        u1 = jnp.where(row == 0, p1, pltpu.roll(u, 1, axis=0))
        u2 = jnp.where(row == 0, p0, jnp.where(row == 1, p1, pltpu.roll(u, 2, axis=0)))
        conv = u2 * w0 + u1 * w1 + u * w2
        mix_s[rs, 0:d_conv] = (hb_s[rs, :] * conv).astype(BF16)
        conv_ref[i] = u[tc - (CONV_WIDTH - 1):tc, :]

    gla_scores_and_updates()
    if side_work is not None:
        side_work()
    u_s[...] = proj(o_hc, d_conv) * proj(o_hin, d_conv)
    gla_state_recurrence()
    hb_s[...] = proj(o_hb, d_conv)
    for i in range(nb):
        conv_segment(i)
    mo = jnp.dot(mix_s[:, 0:d_conv], w_o_ref[0:d_conv, :], preferred_element_type=F32)
    gla_outputs()
    half = rows // 2
    for r in (0, half):
        mo_r = mo[r:r + half] + jnp.dot(mix_s[r:r + half, d_conv:d_model],
                                        w_o_ref[d_conv:d_model, :], preferred_element_type=F32)
        if nb == 1:
            y = alpha * x_ref[0, r:r + half, :] + mod_ref[0, 2] * mo_r
            x1_ref[0, r:r + half, :] = _layernorm(y, ln_g_ref[...], ln_b_ref[...])
        else:
            bs_ = slice(r // tc, (r + half) // tc)
            y = alpha * x_ref[bs_] + mod_ref[bs_, 2] * mo_r.reshape(-1, tc, d_model)
            x1_ref[bs_] = _layernorm(y, ln_g_ref[...], ln_b_ref[...])

    @pl.when(pl.program_id(1) == pl.num_programs(1) - 1)
    def _():
        for i in range(nb):
            for hd in range(GLA_HEADS):
                s_ref[i, hd] = st_s[i, hd].T


def _mixer_cast_kernel(*refs, n_in, n_cast, **static):
    ins, rest = refs[:n_in], refs[n_in:]
    cast_in, rest = rest[:n_cast], rest[n_cast:]
    outs, rest = rest[:3], rest[3:]
    cast_out, scratch = rest[:n_cast], rest[n_cast:]
    def cast_slices():
        for src, dst in zip(cast_in, cast_out):
            dst[...] = src[...].astype(dst.dtype)

    _mixer_kernel(*ins, *outs, *scratch, side_work=cast_slices, **static)


def _cast_block_rows(n_rows, n_steps):
    return next(m for m in range(BF16_ROWS, n_rows + 1, BF16_ROWS)
                if n_rows % m == 0 and m * n_steps >= n_rows)


def _mixer(x, mod, conv0, s0, w_in, w_fl, w_conv, w_f, b_f, gain, w_o, ln_g, ln_b,
           *, nb, tc, chunk, gate_rank, alpha, cast=()):
    bsz, t, d = x.shape
    d_conv = conv0.shape[-1]
    heads, head_k, head_v = s0.shape[1:]
    gla_dk, d_gla = heads * head_k, heads * head_v
    rows = nb * tc
    t_tiles = t // tc
    n_steps = (bsz // nb) * t_tiles
    const2 = lambda bi, ti: (0, 0)
    resident = functools.partial(pl.BlockSpec, index_map=const2, pipeline_mode=pl.Buffered(1))
    static = dict(nb=nb, tc=tc, chunk=chunk, gate_rank=gate_rank, alpha=alpha)
    kern = functools.partial(_mixer_cast_kernel, n_in=13, n_cast=len(cast), **static)

    def cast_spec(w):
        m = _cast_block_rows(w.shape[0], n_steps)
        last = w.shape[0] // m - 1
        return pl.BlockSpec((m, w.shape[1]),
                            lambda bi, ti: (jnp.minimum(bi * t_tiles + ti, last), 0))

    cast_specs = [cast_spec(w) for w in cast]
    return pl.pallas_call(
        kern,
        grid=(bsz // nb, t_tiles),
        in_specs=[
            pl.BlockSpec((nb, tc, d), lambda bi, ti: (bi, ti, 0)),
            pl.BlockSpec((nb, 6, 1, d), lambda bi, ti: (bi, 0, 0, 0)),
            pl.BlockSpec((nb, CONV_WIDTH - 1, d_conv), lambda bi, ti: (bi, 0, 0)),
            pl.BlockSpec((nb, heads, head_k, head_v), lambda bi, ti: (bi, 0, 0, 0)),
            resident(w_in.shape),
            resident(w_fl.shape),
            pl.BlockSpec(w_conv.shape, const2),
            pl.BlockSpec(w_f.shape, const2),
            pl.BlockSpec(b_f.shape, const2),
            pl.BlockSpec(gain.shape, const2),
            resident(w_o.shape),
            pl.BlockSpec(ln_g.shape, const2),
            pl.BlockSpec(ln_b.shape, const2),
        ] + cast_specs,
        out_specs=[
            pl.BlockSpec((nb, tc, d), lambda bi, ti: (bi, ti, 0)),
            pl.BlockSpec((nb, CONV_WIDTH - 1, d_conv), lambda bi, ti: (bi, 0, 0)),
            pl.BlockSpec((nb, heads, head_k, head_v), lambda bi, ti: (bi, 0, 0, 0)),
        ] + cast_specs,
        out_shape=[
            jax.ShapeDtypeStruct((bsz, t, d), F32),
            jax.ShapeDtypeStruct(conv0.shape, F32),
            jax.ShapeDtypeStruct(s0.shape, F32),
        ] + [jax.ShapeDtypeStruct(w.shape, BF16) for w in cast],
        scratch_shapes=[
            pltpu.VMEM((rows, d), BF16),
            pltpu.VMEM((rows, d_conv), F32),
            pltpu.VMEM((rows, d_conv), F32),
            pltpu.VMEM((rows, gla_dk), BF16),
            pltpu.VMEM((rows, gla_dk), BF16),
            pltpu.VMEM((rows, gla_dk), BF16),
            pltpu.VMEM((rows, d_gla), BF16),
            pltpu.VMEM((rows, d_gla), F32),
            pltpu.VMEM((nb, heads, head_v, head_k), F32),
            pltpu.VMEM((rows, heads * chunk), BF16),
            pltpu.VMEM((rows // chunk, heads, head_v, head_k), F32),
            pltpu.VMEM((rows // chunk, heads, head_v, head_k), BF16),
            pltpu.VMEM((rows, d), BF16),
        ],
        compiler_params=pltpu.CompilerParams(dimension_semantics=("arbitrary", "arbitrary"),
                                             vmem_limit_bytes=VMEM_LIMIT),
        name="mixer",
    )(x, mod, conv0, s0, w_in, w_fl, w_conv, w_f, b_f, gain, w_o, ln_g, ln_b, *cast)


def _ffn_kernel(x1_ref, mod_ref, wg_hbm, wu_hbm, wd_hbm, ln_g_ref, ln_b_ref, o_ref,
                h2_s, wg_s, wu_s, wd_s, sems, *, nb, tc, tf, sub, alpha):
    d_model = x1_ref.shape[-1]
    n_f = wg_hbm.shape[1] // tf
    rows = nb * tc
    r = pl.program_id(0)
    first_slot = lax.rem(r * n_f, 2)

    def slot_of(f):
        return lax.rem(first_slot + f, 2)

    def weight_copies(f, slot):
        cols = pl.ds(f * tf if isinstance(f, int) else pl.multiple_of(f * tf, tf), tf)
        return (pltpu.make_async_copy(wg_hbm.at[:, cols], wg_s.at[slot], sems.at[0, slot]),
                pltpu.make_async_copy(wu_hbm.at[:, cols], wu_s.at[slot], sems.at[1, slot]),
                pltpu.make_async_copy(wd_hbm.at[cols, :], wd_s.at[slot], sems.at[2, slot]))

    def start(f, slot):
        for c in weight_copies(f, slot):
            c.start()

    def wait(f, slot):
        for c in weight_copies(f, slot):
            c.wait()

    if nb == 1:
        subs = [(slice(0, 1), slice(q, q + sub), slice(q, q + sub)) for q in range(0, tc, sub)]
    else:
        subs = [(slice(0, nb), slice(0, tc), slice(0, rows))]

    def swiglu(rows_flat, slot):
        h2 = h2_s[rows_flat, :]
        a = jnp.dot(h2, wg_s[slot], preferred_element_type=F32)
        b = jnp.dot(h2, wu_s[slot], preferred_element_type=F32)
        part = jnp.dot((_silu(a) * b).astype(BF16), wd_s[slot], preferred_element_type=F32)
        return part.reshape(-1, tc if nb > 1 else sub, d_model)

    @pl.when(r == 0)
    def _():
        start(0, slot_of(0))

    start(1, slot_of(1))
    wait(0, slot_of(0))
    for bs_, rs_, fs_ in subs:
        h2 = x1_ref[bs_, rs_, :] * (1.0 + mod_ref[bs_, 4]) + mod_ref[bs_, 3]
        h2_s[fs_, :] = h2.reshape(-1, d_model).astype(BF16)
        o_ref[bs_, rs_, :] = swiglu(fs_, slot_of(0))

    def middle(f, carry):
        start(f + 1, slot_of(f + 1))
        wait(f, slot_of(f))
        for bs_, rs_, fs_ in subs:
            o_ref[bs_, rs_, :] += swiglu(fs_, slot_of(f))
        return carry

    lax.fori_loop(1, n_f - 1, middle, 0)

    @pl.when(r + 1 < pl.num_programs(0))
    def _():
        start(0, slot_of(n_f))

    wait(n_f - 1, slot_of(n_f - 1))
    for bs_, rs_, fs_ in subs:
        acc = o_ref[bs_, rs_, :] + swiglu(fs_, slot_of(n_f - 1))
        y = alpha * x1_ref[bs_, rs_, :] + mod_ref[bs_, 5] * acc
        o_ref[bs_, rs_, :] = _layernorm(y, ln_g_ref[...], ln_b_ref[...])


def _ffn(x1, mod, w_gate, w_up, w_down, ln_g, ln_b, *, nb, tc, tf, sub, alpha):
    bsz, t, d = x1.shape
    d_ff = w_gate.shape[1]
    rows = nb * tc
    t_tiles = t // tc
    assert d_ff % tf == 0 and d_ff // tf >= 3, "first, middle and last d_ff tiles are peeled"
    kern = functools.partial(_ffn_kernel, nb=nb, tc=tc, tf=tf, sub=sub, alpha=alpha)
    x_map = lambda ri: (ri // t_tiles, ri % t_tiles, 0)
    hbm = pl.BlockSpec(memory_space=pl.ANY)
    return pl.pallas_call(
        kern,
        grid=((bsz // nb) * t_tiles,),
        in_specs=[
            pl.BlockSpec((nb, tc, d), x_map),
            pl.BlockSpec((nb, 6, 1, d), lambda ri: (ri // t_tiles, 0, 0, 0)),
            hbm, hbm, hbm,
            pl.BlockSpec(ln_g.shape, lambda ri: (0, 0)),
            pl.BlockSpec(ln_b.shape, lambda ri: (0, 0)),
        ],
        out_specs=pl.BlockSpec((nb, tc, d), x_map),
        out_shape=jax.ShapeDtypeStruct((bsz, t, d), F32),
        scratch_shapes=[
            pltpu.VMEM((rows, d), BF16),
            pltpu.VMEM((2, d, tf), BF16),
            pltpu.VMEM((2, d, tf), BF16),
            pltpu.VMEM((2, tf, d), BF16),
            pltpu.SemaphoreType.DMA((3, 2)),
        ],
        compiler_params=pltpu.CompilerParams(dimension_semantics=("arbitrary",),
                                             vmem_limit_bytes=VMEM_LIMIT),
        name="ffn",
    )(x1, mod, w_gate, w_up, w_down, ln_g, ln_b)


def kernel(x_prompt, x_sample, c_prompt, c_sample, cache_conv, state_gla, w_ada, b_ada, w_in,
           w_conv, w_f, b_f, gla_gain, w_o, ln1_g, ln1_b, w_gate, w_up, w_down, ln2_g, ln2_b):
    depth = w_ada.shape[0]
    bp, _, d_model = x_prompt.shape
    bs, t_s, _ = x_sample.shape
    d_conv = cache_conv.shape[-1]
    heads, head_k, head_v = state_gla.shape[2:]
    gate_rank = w_f.shape[1]
    alpha = (2.0 * depth) ** 0.25

    yp, ys = x_prompt, x_sample
    conv_p, gla_p, conv_s, gla_s = [], [], [], []
    for l in range(depth):
        c_all = jnp.concatenate([c_prompt, c_sample], axis=0)
        pad_rows = -c_all.shape[0] % 8
        c_all = jnp.pad(c_all, ((0, pad_rows), (0, 0)))
        mod = _modulation(c_all, w_ada[l], b_ada[l], tn=MOD_TN)
        mod_p = mod[:bp].reshape(bp, 6, 1, d_model)
        mod_s = mod[bp:bp + bs].reshape(bs, 6, 1, d_model)

        n_main = w_in.shape[-1] - gate_rank
        copies = len(GATE_LHS_TERM)
        fill = LANE - copies * gate_rank
        w_in_b = w_in[l].astype(BF16)
        w_fl_b = jnp.concatenate([w_in[l][:, n_main:]] * copies
                                 + [jnp.zeros((d_model, fill), F32)], axis=1).astype(BF16)
        wf_terms = _bf16_terms(w_f[l])
        w_f_p = jnp.concatenate([wf_terms[t] for t in GATE_RHS_TERM]
                                + [jnp.zeros((fill, w_f.shape[-1]), BF16)], axis=0)
        w_o_b = w_o[l].astype(BF16)
        row = lambda a: a.reshape(1, -1)
        mixer_w = (w_in_b, w_fl_b, w_conv[l], w_f_p, row(b_f[l]), row(gla_gain[l]), w_o_b,
                   row(ln1_g[l]), row(ln1_b[l]))

        conv0 = jnp.zeros((bp, CONV_WIDTH - 1, d_conv), F32)
        gla0 = jnp.zeros((bp, heads, head_k, head_v), F32)
        x1p, cp, sp, wg_b, wu_b, wd_b = _mixer(
            yp, mod_p, conv0, gla0, *mixer_w, gate_rank=gate_rank, alpha=alpha,
            cast=(w_gate[l], w_up[l], w_down[l]), **PROMPT_MIXER)
        ffn_w = (wg_b, wu_b, wd_b, row(ln2_g[l]), row(ln2_b[l]))
        yp = _ffn(x1p, mod_p, *ffn_w, alpha=alpha, **PROMPT_FFN)

        x1s, cs, ss = _mixer(ys, mod_s, cache_conv[l], state_gla[l], *mixer_w,
                             nb=SAMPLE_MIXER_NB, tc=t_s, chunk=t_s, gate_rank=gate_rank,
                             alpha=alpha)
        ys = _ffn(x1s, mod_s, *ffn_w, nb=bs, tc=t_s, tf=SAMPLE_FFN_TF, sub=t_s, alpha=alpha)

        conv_p.append(cp)
        gla_p.append(sp)
        conv_s.append(cs)
        gla_s.append(ss)
    return (yp, ys, jnp.stack(conv_p), jnp.stack(gla_p), jnp.stack(conv_s), jnp.stack(gla_s))
```

```python
import functools

import jax
import jax.numpy as jnp
from jax import lax
from jax.experimental import pallas as pl
from jax.experimental.pallas import tpu as pltpu

F32 = jnp.float32
BF16 = jnp.bfloat16

CONV_WIDTH = 3
GLA_HEADS = 4
GATE_TAU = 16.0
LN_EPS = 1e-5
RMS_EPS = 1e-6
LANE = 128
SUBLANES = 8
BF16_ROWS = 16
VMEM_LIMIT = 60 * 1024 * 1024

PROMPT_MIXER = dict(nb=1, tc=256, chunk=64)
PROMPT_FFN = dict(nb=1, tc=1024, tf=512, sub=512)
MOD_TN = 1536
SAMPLE_MIXER_NB = 4
SAMPLE_FFN_TF = 512
GATE_LHS_TERM = (0, 1, 2, 0, 1, 0)
GATE_RHS_TERM = (0, 0, 0, 1, 1, 2)


def _bf16_terms(a):
    t1 = a.astype(BF16)
    r1 = a - t1.astype(F32)
    t2 = r1.astype(BF16)
    t3 = (r1 - t2.astype(F32)).astype(BF16)
    return t1, t2, t3


def _layernorm(y, g, b):
    mu = jnp.mean(y, axis=-1, keepdims=True)
    d = y - mu
    var = jnp.mean(d * d, axis=-1, keepdims=True)
    return d * lax.rsqrt(var + LN_EPS) * g + b


def _silu(a):
    return a * jax.nn.sigmoid(a)


def _mod_kernel(c_ref, w_ref, b_ref, o_ref):
    a = _silu(c_ref[...]).astype(BF16)
    o_ref[...] = jnp.dot(a, w_ref[...].astype(BF16), preferred_element_type=F32) + b_ref[...]


def _modulation(c_all, w_ada, b_ada, *, tn):
    m, d = c_all.shape
    n = w_ada.shape[1]
    return pl.pallas_call(
        _mod_kernel,
        grid=(n // tn,),
        in_specs=[pl.BlockSpec((m, d), lambda j: (0, 0)),
                  pl.BlockSpec((d, tn), lambda j: (0, j)),
                  pl.BlockSpec((1, tn), lambda j: (0, j))],
        out_specs=pl.BlockSpec((m, tn), lambda j: (0, j)),
        out_shape=jax.ShapeDtypeStruct((m, n), F32),
        compiler_params=pltpu.CompilerParams(dimension_semantics=("arbitrary",),
                                             vmem_limit_bytes=VMEM_LIMIT),
        name="adaln_modulation",
    )(c_all, w_ada, b_ada.reshape(1, n))


def _mixer_kernel(x_ref, mod_ref, conv0_ref, s0_ref, w_in_ref, w_fl_ref, w_conv_ref, w_f_ref, b_f_ref,
                  gain_ref, w_o_ref, ln_g_ref, ln_b_ref,
                  x1_ref, conv_ref, s_ref,
                  h_s, hb_s, u_s, b_s, qk_s, dg_s, qs_s, kd_s, v_s, go_s, st_s, a_s, kv_s, stb_s, mix_s,
                  *, nb, tc, chunk, gate_rank, alpha, side_work=None):
    d_model = x_ref.shape[-1]
    d_conv = conv_ref.shape[-1]
    head_k, head_v = s_ref.shape[-2], s_ref.shape[-1]
    gla_dk = GLA_HEADS * head_k
    d_gla = GLA_HEADS * head_v
    rows = nb * tc
    n_chunks = tc // chunk
    tot_chunks = rows // chunk
    levels = [chunk >> i for i in range(1, chunk.bit_length())]
    contract0 = (((0,), (0,)), ((), ()))
    contract1 = (((1,), (1,)), ((), ()))

    @pl.when(pl.program_id(1) == 0)
    def _():
        conv_ref[...] = conv0_ref[...]
        for i in range(nb):
            for hd in range(GLA_HEADS):
                st_s[i, hd] = s0_ref[i, hd].T

    h = x_ref[...] * (1.0 + mod_ref[:, 1]) + mod_ref[:, 0]
    h_s[...] = h.reshape(rows, d_model).astype(BF16)

    def proj(lo, width):
        return jnp.dot(h_s[...], w_in_ref[:, lo:lo + width], preferred_element_type=F32)

    o_hb, o_hc, o_hin = 0, d_conv, 2 * d_conv
    o_q = 3 * d_conv
    o_k = o_q + gla_dk
    o_v = o_k + gla_dk
    o_go = o_v + d_gla

    flr = jnp.dot(h_s[...], w_fl_ref[...], preferred_element_type=F32)
    t1 = flr.astype(BF16).astype(F32)
    t2 = (flr - t1).astype(BF16).astype(F32)
    t3 = (flr - t1 - t2).astype(BF16).astype(F32)
    terms = (t1, t2, t3)
    grp = lax.broadcasted_iota(jnp.int32, flr.shape, 1) // gate_rank
    lhs = terms[0]
    for g, term in enumerate(GATE_LHS_TERM):
        if term:
            lhs = jnp.where(grp == g, terms[term], lhs)
    z = jnp.dot(lhs.astype(BF16), w_f_ref[...], preferred_element_type=F32) + b_f_ref[...]
    q = proj(o_q, gla_dk) * (head_k ** -0.5)
    k = proj(o_k, gla_dk)
    lf = jax.nn.log_sigmoid(z) * (1.0 / GATE_TAU)
    lf1 = lf.astype(BF16)
    rem = lf - lf1.astype(F32)
    lf2 = rem.astype(BF16)
    lf3 = (rem - lf2.astype(F32)).astype(BF16)
    ri = lax.broadcasted_iota(jnp.int32, (rows, rows), 0)
    ci = lax.broadcasted_iota(jnp.int32, (rows, rows), 1)
    ltri = ((ri // chunk == ci // chunk) & (ci <= ri)).astype(BF16)
    parts = jnp.dot(ltri, jnp.concatenate([lf1, lf2, lf3], axis=1), preferred_element_type=F32)
    v_s[...] = proj(o_v, d_gla).astype(BF16)
    b = parts[:, 0:gla_dk] + parts[:, gla_dk:2 * gla_dk] + parts[:, 2 * gla_dk:3 * gla_dk]

    def block_rows(block, r):
        picked = [b[n * block + r:n * block + r + 1, :] for n in range(rows // block)]
        spread = jnp.concatenate([jnp.broadcast_to(e, (block, gla_dk)) for e in picked], axis=0)
        return picked, spread

    b_end, b_tot = block_rows(chunk, chunk - 1)
    chunk_decay = [jnp.exp(e) for e in b_end]
    qs_s[...] = (q * jnp.exp(b)).astype(BF16)
    kd_s[...] = (k * jnp.exp(b_tot - b)).astype(BF16)
    for hd in range(GLA_HEADS):
        ks = slice(hd * head_k, (hd + 1) * head_k)
        dg_s[hd] = jnp.sum(q[:, ks] * k[:, ks], axis=-1, keepdims=True)

    n_lane_tiles = gla_dk // LANE
    for lt in range(n_lane_tiles):
        b_s[lt] = b[:, lt * LANE:(lt + 1) * LANE]
    row_id = lax.broadcasted_iota(jnp.int32, (rows, gla_dk), 0)
    groups = rows // SUBLANES
    sub_id = lax.broadcasted_iota(jnp.int32, (groups, SUBLANES, LANE), 1)

    def level_ref(s):
        if s >= SUBLANES:
            return block_rows(2 * s, s)[1]
        tiles = []
        for lt in range(n_lane_tiles):
            out = None
            for r0 in range(s, SUBLANES, 2 * s):
                pick = b_s[lt, pl.ds(r0, groups, stride=SUBLANES), :]
                cand = jnp.broadcast_to(pick[:, None, :], (groups, SUBLANES, LANE))
                out = cand if out is None else jnp.where(sub_id >= r0 - s, cand, out)
            tiles.append(out.reshape(rows, LANE))
        return jnp.concatenate(tiles, axis=1)

    for lev, s in enumerate(levels):
        upper = (row_id & (2 * s - 1)) >= s
        e = b - level_ref(s)
        decay = jnp.exp(jnp.minimum(jnp.where(upper, e, -e), 0.0))
        qk_s[lev] = (jnp.where(upper, q, k) * decay).astype(BF16)
    go_s[...] = proj(o_go, d_gla)

    cr = lax.broadcasted_iota(jnp.int32, (chunk, chunk), 0)
    cc = lax.broadcasted_iota(jnp.int32, (chunk, chunk), 1)
    diagonal = cr == cc
    level_masks = [((cr & -(2 * s)) == (cc & -(2 * s))) & ((cr & s) != 0) & ((cc & s) == 0)
                   for s in levels]
    gain = gain_ref[...]

    head_chunks = [(n, hd) for n in range(tot_chunks) for hd in range(GLA_HEADS)]

    def slices(n, hd):
        return (slice(n * chunk, (n + 1) * chunk), slice(hd * head_k, (hd + 1) * head_k),
                slice(hd * head_v, (hd + 1) * head_v))

    def gla_scores_and_updates():
        for n, hd in head_chunks:
            rs, ks, vs = slices(n, hd)
            a = jnp.where(diagonal, dg_s[hd, rs, :], 0.0)
            for lev, mask in enumerate(level_masks):
                f = qk_s[lev, rs, ks]
                p = lax.dot_general(f, f, contract1, preferred_element_type=F32)
                a = a + jnp.where(mask, p, 0.0)
            a_s[rs, hd * chunk:(hd + 1) * chunk] = a.astype(BF16)
            kv_s[n, hd] = lax.dot_general(v_s[rs, vs], kd_s[rs, ks], contract0,
                                          preferred_element_type=F32)

    def gla_state_recurrence():
        for n, hd in head_chunks:
            _, ks, _ = slices(n, hd)
            st = st_s[n // n_chunks, hd]
            stb_s[n, hd] = st.astype(BF16)
            st_s[n // n_chunks, hd] = st * chunk_decay[n][:, ks] + kv_s[n, hd]

    def gla_outputs():
        for n, hd in head_chunks:
            rs, ks, vs = slices(n, hd)
            o = (jnp.dot(a_s[rs, hd * chunk:(hd + 1) * chunk], v_s[rs, vs],
                         preferred_element_type=F32)
                 + lax.dot_general(qs_s[rs, ks], stb_s[n, hd], contract1,
                                   preferred_element_type=F32))
            o = o * lax.rsqrt(jnp.mean(o * o, axis=-1, keepdims=True) + RMS_EPS) * gain
            mix_s[rs, d_conv + hd * head_v:d_conv + (hd + 1) * head_v] = (
                o * _silu(go_s[rs, vs])).astype(BF16)

    row = lax.broadcasted_iota(jnp.int32, (tc, d_conv), 0)
    w0, w1, w2 = w_conv_ref[0:1, :], w_conv_ref[1:2, :], w_conv_ref[2:3, :]

    def conv_segment(i):
        rs = slice(i * tc, (i + 1) * tc)
        u = u_s[rs, :]
        st = conv_ref[i]
        p0, p1 = st[0:1, :], st[1:2, :]
        u1 = jnp.where(row == 0, p1, pltpu.roll(u, 1, axis=0))
        u2 = jnp.where(row == 0, p0, jnp.where(row == 1, p1, pltpu.roll(u, 2, axis=0)))
        conv = u2 * w0 + u1 * w1 + u * w2
        mix_s[rs, 0:d_conv] = (hb_s[rs, :] * conv).astype(BF16)
        conv_ref[i] = u[tc - (CONV_WIDTH - 1):tc, :]

    gla_scores_and_updates()
    if side_work is not None:
        side_work()
    u_s[...] = proj(o_hc, d_conv) * proj(o_hin, d_conv)
    gla_state_recurrence()
    hb_s[...] = proj(o_hb, d_conv)
    for i in range(nb):
        conv_segment(i)
    mo = jnp.dot(mix_s[:, 0:d_conv], w_o_ref[0:d_conv, :], preferred_element_type=F32)
    gla_outputs()
    half = rows // 2
    for r in (0, half):
        mo_r = mo[r:r + half] + jnp.dot(mix_s[r:r + half, d_conv:d_model],
                                        w_o_ref[d_conv:d_model, :], preferred_element_type=F32)
        if nb == 1:
            y = alpha * x_ref[0, r:r + half, :] + mod_ref[0, 2] * mo_r
            x1_ref[0, r:r + half, :] = _layernorm(y, ln_g_ref[...], ln_b_ref[...])
        else:
            bs_ = slice(r // tc, (r + half) // tc)
            y = alpha * x_ref[bs_] + mod_ref[bs_, 2] * mo_r.reshape(-1, tc, d_model)
            x1_ref[bs_] = _layernorm(y, ln_g_ref[...], ln_b_ref[...])

    @pl.when(pl.program_id(1) == pl.num_programs(1) - 1)
    def _():
        for i in range(nb):
            for hd in range(GLA_HEADS):
                s_ref[i, hd] = st_s[i, hd].T


def _mixer_cast_kernel(*refs, n_in, n_cast, **static):
    ins, rest = refs[:n_in], refs[n_in:]
    cast_in, rest = rest[:n_cast], rest[n_cast:]
    outs, rest = rest[:3], rest[3:]
    cast_out, scratch = rest[:n_cast], rest[n_cast:]
    def cast_slices():
        for src, dst in zip(cast_in, cast_out):
            dst[...] = src[...].astype(dst.dtype)

    _mixer_kernel(*ins, *outs, *scratch, side_work=cast_slices, **static)


def _cast_block_rows(n_rows, n_steps):
    return next(m for m in range(BF16_ROWS, n_rows + 1, BF16_ROWS)
                if n_rows % m == 0 and m * n_steps >= n_rows)


def _mixer(x, mod, conv0, s0, w_in, w_fl, w_conv, w_f, b_f, gain, w_o, ln_g, ln_b,
           *, nb, tc, chunk, gate_rank, alpha, cast=()):
    bsz, t, d = x.shape
    d_conv = conv0.shape[-1]
    heads, head_k, head_v = s0.shape[1:]
    gla_dk, d_gla = heads * head_k, heads * head_v
    rows = nb * tc
    t_tiles = t // tc
    n_steps = (bsz // nb) * t_tiles
    const2 = lambda bi, ti: (0, 0)
    resident = functools.partial(pl.BlockSpec, index_map=const2, pipeline_mode=pl.Buffered(1))
    static = dict(nb=nb, tc=tc, chunk=chunk, gate_rank=gate_rank, alpha=alpha)
    kern = functools.partial(_mixer_cast_kernel, n_in=13, n_cast=len(cast), **static)

    def cast_spec(w):
        m = _cast_block_rows(w.shape[0], n_steps)
        last = w.shape[0] // m - 1
        return pl.BlockSpec((m, w.shape[1]),
                            lambda bi, ti: (jnp.minimum(bi * t_tiles + ti, last), 0))

    cast_specs = [cast_spec(w) for w in cast]
    return pl.pallas_call(
        kern,
        grid=(bsz // nb, t_tiles),
        in_specs=[
            pl.BlockSpec((nb, tc, d), lambda bi, ti: (bi, ti, 0)),
            pl.BlockSpec((nb, 6, 1, d), lambda bi, ti: (bi, 0, 0, 0)),
            pl.BlockSpec((nb, CONV_WIDTH - 1, d_conv), lambda bi, ti: (bi, 0, 0)),
            pl.BlockSpec((nb, heads, head_k, head_v), lambda bi, ti: (bi, 0, 0, 0)),
            resident(w_in.shape),
            resident(w_fl.shape),
            pl.BlockSpec(w_conv.shape, const2),
            pl.BlockSpec(w_f.shape, const2),
            pl.BlockSpec(b_f.shape, const2),
            pl.BlockSpec(gain.shape, const2),
            resident(w_o.shape),
            pl.BlockSpec(ln_g.shape, const2),
            pl.BlockSpec(ln_b.shape, const2),
        ] + cast_specs,
        out_specs=[
            pl.BlockSpec((nb, tc, d), lambda bi, ti: (bi, ti, 0)),
            pl.BlockSpec((nb, CONV_WIDTH - 1, d_conv), lambda bi, ti: (bi, 0, 0)),
            pl.BlockSpec((nb, heads, head_k, head_v), lambda bi, ti: (bi, 0, 0, 0)),
        ] + cast_specs,
        out_shape=[
            jax.ShapeDtypeStruct((bsz, t, d), F32),
            jax.ShapeDtypeStruct(conv0.shape, F32),
            jax.ShapeDtypeStruct(s0.shape, F32),
        ] + [jax.ShapeDtypeStruct(w.shape, BF16) for w in cast],
        scratch_shapes=[
            pltpu.VMEM((rows, d), BF16),
            pltpu.VMEM((rows, d_conv), F32),
            pltpu.VMEM((rows, d_conv), F32),
            pltpu.VMEM((gla_dk // LANE, rows, LANE), F32),
            pltpu.VMEM((chunk.bit_length() - 1, rows, gla_dk), BF16),
            pltpu.VMEM((heads, rows, 1), F32),
            pltpu.VMEM((rows, gla_dk), BF16),
            pltpu.VMEM((rows, gla_dk), BF16),
            pltpu.VMEM((rows, d_gla), BF16),
            pltpu.VMEM((rows, d_gla), F32),
            pltpu.VMEM((nb, heads, head_v, head_k), F32),
            pltpu.VMEM((rows, heads * chunk), BF16),
            pltpu.VMEM((rows // chunk, heads, head_v, head_k), F32),
            pltpu.VMEM((rows // chunk, heads, head_v, head_k), BF16),
            pltpu.VMEM((rows, d), BF16),
        ],
        compiler_params=pltpu.CompilerParams(dimension_semantics=("arbitrary", "arbitrary"),
                                             vmem_limit_bytes=VMEM_LIMIT),
        name="mixer",
    )(x, mod, conv0, s0, w_in, w_fl, w_conv, w_f, b_f, gain, w_o, ln_g, ln_b, *cast)


def _ffn_kernel(x1_ref, mod_ref, wg_hbm, wu_hbm, wd_hbm, ln_g_ref, ln_b_ref, o_ref,
                h2_s, wg_s, wu_s, wd_s, sems, *, nb, tc, tf, sub, alpha):
    d_model = x1_ref.shape[-1]
    n_f = wg_hbm.shape[1] // tf
    rows = nb * tc
    r = pl.program_id(0)
    first_slot = lax.rem(r * n_f, 2)

    def slot_of(f):
        return lax.rem(first_slot + f, 2)

    def weight_copies(f, slot):
        cols = pl.ds(f * tf if isinstance(f, int) else pl.multiple_of(f * tf, tf), tf)
        return (pltpu.make_async_copy(wg_hbm.at[:, cols], wg_s.at[slot], sems.at[0, slot]),
                pltpu.make_async_copy(wu_hbm.at[:, cols], wu_s.at[slot], sems.at[1, slot]),
                pltpu.make_async_copy(wd_hbm.at[cols, :], wd_s.at[slot], sems.at[2, slot]))

    def start(f, slot):
        for c in weight_copies(f, slot):
            c.start()

    def wait(f, slot):
        for c in weight_copies(f, slot):
            c.wait()

    if nb == 1:
        subs = [(slice(0, 1), slice(q, q + sub), slice(q, q + sub)) for q in range(0, tc, sub)]
    else:
        subs = [(slice(0, nb), slice(0, tc), slice(0, rows))]

    def swiglu(rows_flat, slot):
        h2 = h2_s[rows_flat, :]
        a = jnp.dot(h2, wg_s[slot], preferred_element_type=F32)
        b = jnp.dot(h2, wu_s[slot], preferred_element_type=F32)
        part = jnp.dot((_silu(a) * b).astype(BF16), wd_s[slot], preferred_element_type=F32)
        return part.reshape(-1, tc if nb > 1 else sub, d_model)

    @pl.when(r == 0)
    def _():
        start(0, slot_of(0))

    start(1, slot_of(1))
    wait(0, slot_of(0))
    for bs_, rs_, fs_ in subs:
        h2 = x1_ref[bs_, rs_, :] * (1.0 + mod_ref[bs_, 4]) + mod_ref[bs_, 3]
        h2_s[fs_, :] = h2.reshape(-1, d_model).astype(BF16)
        o_ref[bs_, rs_, :] = swiglu(fs_, slot_of(0))

    def middle(f, carry):
        start(f + 1, slot_of(f + 1))
        wait(f, slot_of(f))
        for bs_, rs_, fs_ in subs:
            o_ref[bs_, rs_, :] += swiglu(fs_, slot_of(f))
        return carry

    lax.fori_loop(1, n_f - 1, middle, 0)

    @pl.when(r + 1 < pl.num_programs(0))
    def _():
        start(0, slot_of(n_f))

    wait(n_f - 1, slot_of(n_f - 1))
    for bs_, rs_, fs_ in subs:
        acc = o_ref[bs_, rs_, :] + swiglu(fs_, slot_of(n_f - 1))
        y = alpha * x1_ref[bs_, rs_, :] + mod_ref[bs_, 5] * acc
        o_ref[bs_, rs_, :] = _layernorm(y, ln_g_ref[...], ln_b_ref[...])


def _ffn(x1, mod, w_gate, w_up, w_down, ln_g, ln_b, *, nb, tc, tf, sub, alpha):
    bsz, t, d = x1.shape
    d_ff = w_gate.shape[1]
    rows = nb * tc
    t_tiles = t // tc
    assert d_ff % tf == 0 and d_ff // tf >= 3, "first, middle and last d_ff tiles are peeled"
    kern = functools.partial(_ffn_kernel, nb=nb, tc=tc, tf=tf, sub=sub, alpha=alpha)
    x_map = lambda ri: (ri // t_tiles, ri % t_tiles, 0)
    hbm = pl.BlockSpec(memory_space=pl.ANY)
    return pl.pallas_call(
        kern,
        grid=((bsz // nb) * t_tiles,),
        in_specs=[
            pl.BlockSpec((nb, tc, d), x_map),
            pl.BlockSpec((nb, 6, 1, d), lambda ri: (ri // t_tiles, 0, 0, 0)),
            hbm, hbm, hbm,
            pl.BlockSpec(ln_g.shape, lambda ri: (0, 0)),
            pl.BlockSpec(ln_b.shape, lambda ri: (0, 0)),
        ],
        out_specs=pl.BlockSpec((nb, tc, d), x_map),
        out_shape=jax.ShapeDtypeStruct((bsz, t, d), F32),
        scratch_shapes=[
            pltpu.VMEM((rows, d), BF16),
            pltpu.VMEM((2, d, tf), BF16),
            pltpu.VMEM((2, d, tf), BF16),
            pltpu.VMEM((2, tf, d), BF16),
            pltpu.SemaphoreType.DMA((3, 2)),
        ],
        compiler_params=pltpu.CompilerParams(dimension_semantics=("arbitrary",),
                                             vmem_limit_bytes=VMEM_LIMIT),
        name="ffn",
    )(x1, mod, w_gate, w_up, w_down, ln_g, ln_b)


def kernel(x_prompt, x_sample, c_prompt, c_sample, cache_conv, state_gla, w_ada, b_ada, w_in,
           w_conv, w_f, b_f, gla_gain, w_o, ln1_g, ln1_b, w_gate, w_up, w_down, ln2_g, ln2_b):
    depth = w_ada.shape[0]
    bp, _, d_model = x_prompt.shape
    bs, t_s, _ = x_sample.shape
    d_conv = cache_conv.shape[-1]
    heads, head_k, head_v = state_gla.shape[2:]
    gate_rank = w_f.shape[1]
    alpha = (2.0 * depth) ** 0.25

    yp, ys = x_prompt, x_sample
    conv_p, gla_p, conv_s, gla_s = [], [], [], []
    for l in range(depth):
        c_all = jnp.concatenate([c_prompt, c_sample], axis=0)
        pad_rows = -c_all.shape[0] % 8
        c_all = jnp.pad(c_all, ((0, pad_rows), (0, 0)))
        mod = _modulation(c_all, w_ada[l], b_ada[l], tn=MOD_TN)
        mod_p = mod[:bp].reshape(bp, 6, 1, d_model)
        mod_s = mod[bp:bp + bs].reshape(bs, 6, 1, d_model)

        n_main = w_in.shape[-1] - gate_rank
        copies = len(GATE_LHS_TERM)
        fill = LANE - copies * gate_rank
        w_in_b = w_in[l].astype(BF16)
        w_fl_b = jnp.concatenate([w_in[l][:, n_main:]] * copies
                                 + [jnp.zeros((d_model, fill), F32)], axis=1).astype(BF16)
        wf_terms = _bf16_terms(w_f[l])
        w_f_p = jnp.concatenate([wf_terms[t] for t in GATE_RHS_TERM]
                                + [jnp.zeros((fill, w_f.shape[-1]), BF16)], axis=0)
        w_o_b = w_o[l].astype(BF16)
        row = lambda a: a.reshape(1, -1)
        mixer_w = (w_in_b, w_fl_b, w_conv[l], w_f_p, row(b_f[l]), row(gla_gain[l]), w_o_b,
                   row(ln1_g[l]), row(ln1_b[l]))

        conv0 = jnp.zeros((bp, CONV_WIDTH - 1, d_conv), F32)
        gla0 = jnp.zeros((bp, heads, head_k, head_v), F32)
        x1p, cp, sp, wg_b, wu_b, wd_b = _mixer(
            yp, mod_p, conv0, gla0, *mixer_w, gate_rank=gate_rank, alpha=alpha,
            cast=(w_gate[l], w_up[l], w_down[l]), **PROMPT_MIXER)
        ffn_w = (wg_b, wu_b, wd_b, row(ln2_g[l]), row(ln2_b[l]))
        yp = _ffn(x1p, mod_p, *ffn_w, alpha=alpha, **PROMPT_FFN)

        x1s, cs, ss = _mixer(ys, mod_s, cache_conv[l], state_gla[l], *mixer_w,
                             nb=SAMPLE_MIXER_NB, tc=t_s, chunk=t_s, gate_rank=gate_rank,
                             alpha=alpha)
        ys = _ffn(x1s, mod_s, *ffn_w, nb=bs, tc=t_s, tf=SAMPLE_FFN_TF, sub=t_s, alpha=alpha)

        conv_p.append(cp)
        gla_p.append(sp)
        conv_s.append(cs)
        gla_s.append(ss)
    return (yp, ys, jnp.stack(conv_p), jnp.stack(gla_p), jnp.stack(conv_s), jnp.stack(gla_s))
```

```python
import functools

import jax
import jax.numpy as jnp
from jax import lax
from jax.experimental import pallas as pl
from jax.experimental.pallas import tpu as pltpu

F32 = jnp.float32
BF16 = jnp.bfloat16

CONV_WIDTH = 3
GLA_HEADS = 4
GATE_TAU = 16.0
LN_EPS = 1e-5
RMS_EPS = 1e-6
LANE = 128
SUBLANES = 8
BF16_ROWS = 16
VMEM_LIMIT = 60 * 1024 * 1024

PROMPT_MIXER = dict(nb=1, tc=256, chunk=64)
PROMPT_FFN = dict(nb=1, tc=1024, tf=512, sub=512)
MOD_TN = 1536
SAMPLE_MIXER_NB = 4
SAMPLE_FFN_TF = 512
GATE_LHS_TERM = (0, 1, 2, 0, 1, 0)
GATE_RHS_TERM = (0, 0, 0, 1, 1, 2)


def _bf16_terms(a):
    t1 = a.astype(BF16)
    r1 = a - t1.astype(F32)
    t2 = r1.astype(BF16)
    t3 = (r1 - t2.astype(F32)).astype(BF16)
    return t1, t2, t3


def _residual_layernorm(x, gate, branch, g, b, alpha):
    y = x + (gate * (1.0 / alpha)) * branch
    mu = jnp.mean(y, axis=-1, keepdims=True)
    d = y - mu
    var = jnp.mean(d * d, axis=-1, keepdims=True)
    return d * lax.rsqrt(var + LN_EPS / (alpha * alpha)) * g + b


def _silu(a):
    return a * jax.nn.sigmoid(a)


def _mod_kernel(c_ref, w_ref, b_ref, o_ref):
    a = _silu(c_ref[...]).astype(BF16)
    o_ref[...] = jnp.dot(a, w_ref[...].astype(BF16), preferred_element_type=F32) + b_ref[...]


def _modulation(c_all, w_ada, b_ada, *, tn):
    m, d = c_all.shape
    n = w_ada.shape[1]
    return pl.pallas_call(
        _mod_kernel,
        grid=(n // tn,),
        in_specs=[pl.BlockSpec((m, d), lambda j: (0, 0)),
                  pl.BlockSpec((d, tn), lambda j: (0, j)),
                  pl.BlockSpec((1, tn), lambda j: (0, j))],
        out_specs=pl.BlockSpec((m, tn), lambda j: (0, j)),
        out_shape=jax.ShapeDtypeStruct((m, n), F32),
        compiler_params=pltpu.CompilerParams(dimension_semantics=("arbitrary",),
                                             vmem_limit_bytes=VMEM_LIMIT),
        name="adaln_modulation",
    )(c_all, w_ada, b_ada.reshape(1, n))


def _mixer_kernel(x_ref, mod_ref, conv0_ref, s0_ref, w_in_ref, w_fl_ref, w_conv_ref, w_f_ref, b_f_ref,
                  gain_ref, w_o_ref, ln_g_ref, ln_b_ref,
                  x1_ref, conv_ref, s_ref,
                  h_s, hb_s, u_s, b_s, qk_s, dg_s, qs_s, kd_s, v_s, go_s, st_s, a_s, kv_s, stb_s, mix_s,
                  *, nb, tc, chunk, gate_rank, alpha, side_work=None):
    d_model = x_ref.shape[-1]
    d_conv = conv_ref.shape[-1]
    head_k, head_v = s_ref.shape[-2], s_ref.shape[-1]
    gla_dk = GLA_HEADS * head_k
    d_gla = GLA_HEADS * head_v
    rows = nb * tc
    n_chunks = tc // chunk
    tot_chunks = rows // chunk
    levels = [chunk >> i for i in range(1, chunk.bit_length())]
    contract0 = (((0,), (0,)), ((), ()))
    contract1 = (((1,), (1,)), ((), ()))

    @pl.when(pl.program_id(1) == 0)
    def _():
        conv_ref[...] = conv0_ref[...]
        for i in range(nb):
            for hd in range(GLA_HEADS):
                st_s[i, hd] = s0_ref[i, hd].T

    h = x_ref[...] * (1.0 + mod_ref[:, 1]) + mod_ref[:, 0]
    h_s[...] = h.reshape(rows, d_model).astype(BF16)

    def proj(lo, width):
        return jnp.dot(h_s[...], w_in_ref[:, lo:lo + width], preferred_element_type=F32)

    o_hb, o_hc, o_hin = 0, d_conv, 2 * d_conv
    o_q = 3 * d_conv
    o_k = o_q + gla_dk
    o_v = o_k + gla_dk
    o_go = o_v + d_gla

    flr = jnp.dot(h_s[...], w_fl_ref[...], preferred_element_type=F32)
    t1 = flr.astype(BF16).astype(F32)
    t2 = (flr - t1).astype(BF16).astype(F32)
    t3 = (flr - t1 - t2).astype(BF16).astype(F32)
    terms = (t1, t2, t3)
    grp = lax.broadcasted_iota(jnp.int32, flr.shape, 1) // gate_rank
    lhs = terms[0]
    for g, term in enumerate(GATE_LHS_TERM):
        if term:
            lhs = jnp.where(grp == g, terms[term], lhs)
    z = jnp.dot(lhs.astype(BF16), w_f_ref[...], preferred_element_type=F32) + b_f_ref[...]
    q = proj(o_q, gla_dk) * (head_k ** -0.5)
    k = proj(o_k, gla_dk)
    lf = jax.nn.log_sigmoid(z) * (1.0 / GATE_TAU)
    lf1 = lf.astype(BF16)
    rem = lf - lf1.astype(F32)
    lf2 = rem.astype(BF16)
    lf3 = (rem - lf2.astype(F32)).astype(BF16)
    ri = lax.broadcasted_iota(jnp.int32, (rows, rows), 0)
    ci = lax.broadcasted_iota(jnp.int32, (rows, rows), 1)
    ltri = ((ri // chunk == ci // chunk) & (ci <= ri)).astype(BF16)
    parts = jnp.dot(ltri, jnp.concatenate([lf1, lf2, lf3], axis=1), preferred_element_type=F32)
    v_s[...] = proj(o_v, d_gla).astype(BF16)
    b = parts[:, 0:gla_dk] + parts[:, gla_dk:2 * gla_dk] + parts[:, 2 * gla_dk:3 * gla_dk]

    def block_rows(block, r):
        picked = [b[n * block + r:n * block + r + 1, :] for n in range(rows // block)]
        spread = jnp.concatenate([jnp.broadcast_to(e, (block, gla_dk)) for e in picked], axis=0)
        return picked, spread

    b_end, b_tot = block_rows(chunk, chunk - 1)
    chunk_decay = [jnp.exp(e) for e in b_end]
    qs_s[...] = (q * jnp.exp(b)).astype(BF16)
    kd_s[...] = (k * jnp.exp(b_tot - b)).astype(BF16)
    for hd in range(GLA_HEADS):
        ks = slice(hd * head_k, (hd + 1) * head_k)
        dg_s[hd] = jnp.sum(q[:, ks] * k[:, ks], axis=-1, keepdims=True)

    n_lane_tiles = gla_dk // LANE
    for lt in range(n_lane_tiles):
        b_s[lt] = b[:, lt * LANE:(lt + 1) * LANE]
    row_id = lax.broadcasted_iota(jnp.int32, (rows, gla_dk), 0)
    groups = rows // SUBLANES
    sub_id = lax.broadcasted_iota(jnp.int32, (groups, SUBLANES, LANE), 1)

    def level_ref(s):
        if s >= SUBLANES:
            return block_rows(2 * s, s)[1]
        tiles = []
        for lt in range(n_lane_tiles):
            out = None
            for r0 in range(s, SUBLANES, 2 * s):
                pick = b_s[lt, pl.ds(r0, groups, stride=SUBLANES), :]
                cand = jnp.broadcast_to(pick[:, None, :], (groups, SUBLANES, LANE))
                out = cand if out is None else jnp.where(sub_id >= r0 - s, cand, out)
            tiles.append(out.reshape(rows, LANE))
        return jnp.concatenate(tiles, axis=1)

    for lev, s in enumerate(levels):
        upper = (row_id & (2 * s - 1)) >= s
        e = b - level_ref(s)
        decay = jnp.exp(jnp.minimum(jnp.where(upper, e, -e), 0.0))
        qk_s[lev] = (jnp.where(upper, q, k) * decay).astype(BF16)
    go_s[...] = proj(o_go, d_gla)

    cr = lax.broadcasted_iota(jnp.int32, (chunk, chunk), 0)
    cc = lax.broadcasted_iota(jnp.int32, (chunk, chunk), 1)
    diagonal = cr == cc
    level_masks = [((cr & -(2 * s)) == (cc & -(2 * s))) & ((cr & s) != 0) & ((cc & s) == 0)
                   for s in levels]
    gain = gain_ref[...]

    head_chunks = [(n, hd) for n in range(tot_chunks) for hd in range(GLA_HEADS)]

    def slices(n, hd):
        return (slice(n * chunk, (n + 1) * chunk), slice(hd * head_k, (hd + 1) * head_k),
                slice(hd * head_v, (hd + 1) * head_v))

    def gla_scores_and_updates():
        for n, hd in head_chunks:
            rs, ks, vs = slices(n, hd)
            a = jnp.where(diagonal, dg_s[hd, rs, :], 0.0)
            for lev, mask in enumerate(level_masks):
                f = qk_s[lev, rs, ks]
                p = lax.dot_general(f, f, contract1, preferred_element_type=F32)
                a = jnp.where(mask, p, a)
            a_s[rs, hd * chunk:(hd + 1) * chunk] = a.astype(BF16)
            kv_s[n, hd] = lax.dot_general(v_s[rs, vs], kd_s[rs, ks], contract0,
                                          preferred_element_type=F32)

    def gla_state_recurrence():
        for n, hd in head_chunks:
            _, ks, _ = slices(n, hd)
            st = st_s[n // n_chunks, hd]
            stb_s[n, hd] = st.T.astype(BF16)
            st_s[n // n_chunks, hd] = st * chunk_decay[n][:, ks] + kv_s[n, hd]

    def gla_outputs():
        for n, hd in head_chunks:
            rs, ks, vs = slices(n, hd)
            lhs = jnp.concatenate([qs_s[rs, ks], a_s[rs, hd * chunk:(hd + 1) * chunk]], axis=1)
            rhs = jnp.concatenate([stb_s[n, hd], v_s[rs, vs]], axis=0)
            o = jnp.dot(lhs, rhs, preferred_element_type=F32)
            o = o * lax.rsqrt(jnp.mean(o * o, axis=-1, keepdims=True) + RMS_EPS) * gain
            mix_s[rs, d_conv + hd * head_v:d_conv + (hd + 1) * head_v] = (
                o * _silu(go_s[rs, vs])).astype(BF16)

    row = lax.broadcasted_iota(jnp.int32, (tc, d_conv), 0)
    w0, w1, w2 = w_conv_ref[0:1, :], w_conv_ref[1:2, :], w_conv_ref[2:3, :]

    def conv_segment(i):
        rs = slice(i * tc, (i + 1) * tc)
        u = u_s[rs, :]
        st = conv_ref[i]
        p0, p1 = st[0:1, :], st[1:2, :]
        u1 = jnp.where(row == 0, p1, pltpu.roll(u, 1, axis=0))
        u2 = jnp.where(row == 0, p0, jnp.where(row == 1, p1, pltpu.roll(u, 2, axis=0)))
        conv = u2 * w0 + u1 * w1 + u * w2
        mix_s[rs, 0:d_conv] = (hb_s[rs, :] * conv).astype(BF16)
        conv_ref[i] = u[tc - (CONV_WIDTH - 1):tc, :]

    gla_scores_and_updates()
    if side_work is not None:
        side_work()
    u_s[...] = proj(o_hc, d_conv) * proj(o_hin, d_conv)
    gla_state_recurrence()
    hb_s[...] = proj(o_hb, d_conv)
    for i in range(nb):
        conv_segment(i)
    mo = jnp.dot(mix_s[:, 0:d_conv], w_o_ref[0:d_conv, :], preferred_element_type=F32)
    gla_outputs()
    half = rows // 2
    for r in (0, half):
        mo_r = mo[r:r + half] + jnp.dot(mix_s[r:r + half, d_conv:d_model],
                                        w_o_ref[d_conv:d_model, :], preferred_element_type=F32)
        if nb == 1:
            x1_ref[0, r:r + half, :] = _residual_layernorm(
                x_ref[0, r:r + half, :], mod_ref[0, 2], mo_r, ln_g_ref[...], ln_b_ref[...], alpha)
        else:
            bs_ = slice(r // tc, (r + half) // tc)
            x1_ref[bs_] = _residual_layernorm(
                x_ref[bs_], mod_ref[bs_, 2], mo_r.reshape(-1, tc, d_model),
                ln_g_ref[...], ln_b_ref[...], alpha)

    @pl.when(pl.program_id(1) == pl.num_programs(1) - 1)
    def _():
        for i in range(nb):
            for hd in range(GLA_HEADS):
                s_ref[i, hd] = st_s[i, hd].T


def _mixer_cast_kernel(*refs, n_in, n_cast, **static):
    ins, rest = refs[:n_in], refs[n_in:]
    cast_in, rest = rest[:n_cast], rest[n_cast:]
    outs, rest = rest[:3], rest[3:]
    cast_out, scratch = rest[:n_cast], rest[n_cast:]
    def cast_slices():
        for src, dst in zip(cast_in, cast_out):
            dst[...] = src[...].astype(dst.dtype)

    _mixer_kernel(*ins, *outs, *scratch, side_work=cast_slices, **static)


def _cast_block_rows(n_rows, n_steps):
    return next(m for m in range(BF16_ROWS, n_rows + 1, BF16_ROWS)
                if n_rows % m == 0 and m * n_steps >= n_rows)


def _mixer(x, mod, conv0, s0, w_in, w_fl, w_conv, w_f, b_f, gain, w_o, ln_g, ln_b,
           *, nb, tc, chunk, gate_rank, alpha, cast=()):
    bsz, t, d = x.shape
    d_conv = conv0.shape[-1]
    heads, head_k, head_v = s0.shape[1:]
    gla_dk, d_gla = heads * head_k, heads * head_v
    rows = nb * tc
    t_tiles = t // tc
    n_steps = (bsz // nb) * t_tiles
    const2 = lambda bi, ti: (0, 0)
    resident = functools.partial(pl.BlockSpec, index_map=const2, pipeline_mode=pl.Buffered(1))
    static = dict(nb=nb, tc=tc, chunk=chunk, gate_rank=gate_rank, alpha=alpha)
    kern = functools.partial(_mixer_cast_kernel, n_in=13, n_cast=len(cast), **static)

    def cast_spec(w):
        m = _cast_block_rows(w.shape[0], n_steps)
        last = w.shape[0] // m - 1
        return pl.BlockSpec((m, w.shape[1]),
                            lambda bi, ti: (jnp.minimum(bi * t_tiles + ti, last), 0))

    cast_specs = [cast_spec(w) for w in cast]
    return pl.pallas_call(
        kern,
        grid=(bsz // nb, t_tiles),
        in_specs=[
            pl.BlockSpec((nb, tc, d), lambda bi, ti: (bi, ti, 0)),
            pl.BlockSpec((nb, 6, 1, d), lambda bi, ti: (bi, 0, 0, 0)),
            pl.BlockSpec((nb, CONV_WIDTH - 1, d_conv), lambda bi, ti: (bi, 0, 0)),
            pl.BlockSpec((nb, heads, head_k, head_v), lambda bi, ti: (bi, 0, 0, 0)),
            resident(w_in.shape),
            resident(w_fl.shape),
            pl.BlockSpec(w_conv.shape, const2),
            pl.BlockSpec(w_f.shape, const2),
            pl.BlockSpec(b_f.shape, const2),
            pl.BlockSpec(gain.shape, const2),
            resident(w_o.shape),
            pl.BlockSpec(ln_g.shape, const2),
            pl.BlockSpec(ln_b.shape, const2),
        ] + cast_specs,
        out_specs=[
            pl.BlockSpec((nb, tc, d), lambda bi, ti: (bi, ti, 0)),
            pl.BlockSpec((nb, CONV_WIDTH - 1, d_conv), lambda bi, ti: (bi, 0, 0)),
            pl.BlockSpec((nb, heads, head_k, head_v), lambda bi, ti: (bi, 0, 0, 0)),
        ] + cast_specs,
        out_shape=[
            jax.ShapeDtypeStruct((bsz, t, d), F32),
            jax.ShapeDtypeStruct(conv0.shape, F32),
            jax.ShapeDtypeStruct(s0.shape, F32),
        ] + [jax.ShapeDtypeStruct(w.shape, BF16) for w in cast],
        scratch_shapes=[
            pltpu.VMEM((rows, d), BF16),
            pltpu.VMEM((rows, d_conv), F32),
            pltpu.VMEM((rows, d_conv), F32),
            pltpu.VMEM((gla_dk // LANE, rows, LANE), F32),
            pltpu.VMEM((chunk.bit_length() - 1, rows, gla_dk), BF16),
            pltpu.VMEM((heads, rows, 1), F32),
            pltpu.VMEM((rows, gla_dk), BF16),
            pltpu.VMEM((rows, gla_dk), BF16),
            pltpu.VMEM((rows, d_gla), BF16),
            pltpu.VMEM((rows, d_gla), F32),
            pltpu.VMEM((nb, heads, head_v, head_k), F32),
            pltpu.VMEM((rows, heads * chunk), BF16),
            pltpu.VMEM((rows // chunk, heads, head_v, head_k), F32),
            pltpu.VMEM((rows // chunk, heads, head_k, head_v), BF16),
            pltpu.VMEM((rows, d), BF16),
        ],
        compiler_params=pltpu.CompilerParams(dimension_semantics=("arbitrary", "arbitrary"),
                                             vmem_limit_bytes=VMEM_LIMIT),
        name="mixer",
    )(x, mod, conv0, s0, w_in, w_fl, w_conv, w_f, b_f, gain, w_o, ln_g, ln_b, *cast)


def _ffn_kernel(x1_ref, mod_ref, wg_hbm, wu_hbm, wd_hbm, ln_g_ref, ln_b_ref, o_ref,
                h2_s, wg_s, wu_s, wd_s, sems, *, nb, tc, tf, sub, alpha):
    d_model = x1_ref.shape[-1]
    n_f = wg_hbm.shape[1] // tf
    rows = nb * tc
    r = pl.program_id(0)
    first_slot = lax.rem(r * n_f, 2)

    def slot_of(f):
        return lax.rem(first_slot + f, 2)

    def weight_copies(f, slot):
        cols = pl.ds(f * tf if isinstance(f, int) else pl.multiple_of(f * tf, tf), tf)
        return (pltpu.make_async_copy(wg_hbm.at[:, cols], wg_s.at[slot], sems.at[0, slot]),
                pltpu.make_async_copy(wu_hbm.at[:, cols], wu_s.at[slot], sems.at[1, slot]),
                pltpu.make_async_copy(wd_hbm.at[cols, :], wd_s.at[slot], sems.at[2, slot]))

    def start(f, slot):
        for c in weight_copies(f, slot):
            c.start()

    def wait(f, slot):
        for c in weight_copies(f, slot):
            c.wait()

    if nb == 1:
        subs = [(slice(0, 1), slice(q, q + sub), slice(q, q + sub)) for q in range(0, tc, sub)]
    else:
        subs = [(slice(0, nb), slice(0, tc), slice(0, rows))]

    def swiglu(rows_flat, slot):
        h2 = h2_s[rows_flat, :]
        a = jnp.dot(h2, wg_s[slot], preferred_element_type=F32)
        b = jnp.dot(h2, wu_s[slot], preferred_element_type=F32)
        part = jnp.dot((_silu(a) * b).astype(BF16), wd_s[slot], preferred_element_type=F32)
        return part.reshape(-1, tc if nb > 1 else sub, d_model)

    @pl.when(r == 0)
    def _():
        start(0, slot_of(0))

    start(1, slot_of(1))
    wait(0, slot_of(0))
    for bs_, rs_, fs_ in subs:
        h2 = x1_ref[bs_, rs_, :] * (1.0 + mod_ref[bs_, 4]) + mod_ref[bs_, 3]
        h2_s[fs_, :] = h2.reshape(-1, d_model).astype(BF16)
        o_ref[bs_, rs_, :] = swiglu(fs_, slot_of(0))

    def middle(f, carry):
        start(f + 1, slot_of(f + 1))
        wait(f, slot_of(f))
        for bs_, rs_, fs_ in subs:
            o_ref[bs_, rs_, :] += swiglu(fs_, slot_of(f))
        return carry

    lax.fori_loop(1, n_f - 1, middle, 0)

    @pl.when(r + 1 < pl.num_programs(0))
    def _():
        start(0, slot_of(n_f))

    wait(n_f - 1, slot_of(n_f - 1))
    for bs_, rs_, fs_ in subs:
        acc = o_ref[bs_, rs_, :] + swiglu(fs_, slot_of(n_f - 1))
        o_ref[bs_, rs_, :] = _residual_layernorm(
            x1_ref[bs_, rs_, :], mod_ref[bs_, 5], acc, ln_g_ref[...], ln_b_ref[...], alpha)


def _ffn(x1, mod, w_gate, w_up, w_down, ln_g, ln_b, *, nb, tc, tf, sub, alpha):
    bsz, t, d = x1.shape
    d_ff = w_gate.shape[1]
    rows = nb * tc
    t_tiles = t // tc
    assert d_ff % tf == 0 and d_ff // tf >= 3, "first, middle and last d_ff tiles are peeled"
    kern = functools.partial(_ffn_kernel, nb=nb, tc=tc, tf=tf, sub=sub, alpha=alpha)
    x_map = lambda ri: (ri // t_tiles, ri % t_tiles, 0)
    hbm = pl.BlockSpec(memory_space=pl.ANY)
    return pl.pallas_call(
        kern,
        grid=((bsz // nb) * t_tiles,),
        in_specs=[
            pl.BlockSpec((nb, tc, d), x_map),
            pl.BlockSpec((nb, 6, 1, d), lambda ri: (ri // t_tiles, 0, 0, 0)),
            hbm, hbm, hbm,
            pl.BlockSpec(ln_g.shape, lambda ri: (0, 0)),
            pl.BlockSpec(ln_b.shape, lambda ri: (0, 0)),
        ],
        out_specs=pl.BlockSpec((nb, tc, d), x_map),
        out_shape=jax.ShapeDtypeStruct((bsz, t, d), F32),
        scratch_shapes=[
            pltpu.VMEM((rows, d), BF16),
            pltpu.VMEM((2, d, tf), BF16),
            pltpu.VMEM((2, d, tf), BF16),
            pltpu.VMEM((2, tf, d), BF16),
            pltpu.SemaphoreType.DMA((3, 2)),
        ],
        compiler_params=pltpu.CompilerParams(dimension_semantics=("arbitrary",),
                                             vmem_limit_bytes=VMEM_LIMIT),
        name="ffn",
    )(x1, mod, w_gate, w_up, w_down, ln_g, ln_b)


def kernel(x_prompt, x_sample, c_prompt, c_sample, cache_conv, state_gla, w_ada, b_ada, w_in,
           w_conv, w_f, b_f, gla_gain, w_o, ln1_g, ln1_b, w_gate, w_up, w_down, ln2_g, ln2_b):
    depth = w_ada.shape[0]
    bp, _, d_model = x_prompt.shape
    bs, t_s, _ = x_sample.shape
    d_conv = cache_conv.shape[-1]
    heads, head_k, head_v = state_gla.shape[2:]
    gate_rank = w_f.shape[1]
    alpha = (2.0 * depth) ** 0.25

    yp, ys = x_prompt, x_sample
    conv_p, gla_p, conv_s, gla_s = [], [], [], []
    for l in range(depth):
        c_all = jnp.concatenate([c_prompt, c_sample], axis=0)
        pad_rows = -c_all.shape[0] % 8
        c_all = jnp.pad(c_all, ((0, pad_rows), (0, 0)))
        mod = _modulation(c_all, w_ada[l], b_ada[l], tn=MOD_TN)
        mod_p = mod[:bp].reshape(bp, 6, 1, d_model)
        mod_s = mod[bp:bp + bs].reshape(bs, 6, 1, d_model)

        n_main = w_in.shape[-1] - gate_rank
        copies = len(GATE_LHS_TERM)
        fill = LANE - copies * gate_rank
        w_in_b = w_in[l].astype(BF16)
        w_fl_b = jnp.concatenate([w_in[l][:, n_main:]] * copies
                                 + [jnp.zeros((d_model, fill), F32)], axis=1).astype(BF16)
        wf_terms = _bf16_terms(w_f[l])
        w_f_p = jnp.concatenate([wf_terms[t] for t in GATE_RHS_TERM]
                                + [jnp.zeros((fill, w_f.shape[-1]), BF16)], axis=0)
        w_o_b = w_o[l].astype(BF16)
        row = lambda a: a.reshape(1, -1)
        mixer_w = (w_in_b, w_fl_b, w_conv[l], w_f_p, row(b_f[l]), row(gla_gain[l]), w_o_b,
                   row(ln1_g[l]), row(ln1_b[l]))

        conv0 = jnp.zeros((bp, CONV_WIDTH - 1, d_conv), F32)
        gla0 = jnp.zeros((bp, heads, head_k, head_v), F32)
        x1p, cp, sp, wg_b, wu_b, wd_b = _mixer(
            yp, mod_p, conv0, gla0, *mixer_w, gate_rank=gate_rank, alpha=alpha,
            cast=(w_gate[l], w_up[l], w_down[l]), **PROMPT_MIXER)
        ffn_w = (wg_b, wu_b, wd_b, row(ln2_g[l]), row(ln2_b[l]))
        yp = _ffn(x1p, mod_p, *ffn_w, alpha=alpha, **PROMPT_FFN)

        x1s, cs, ss = _mixer(ys, mod_s, cache_conv[l], state_gla[l], *mixer_w,
                             nb=SAMPLE_MIXER_NB, tc=t_s, chunk=t_s, gate_rank=gate_rank,
                             alpha=alpha)
        ys = _ffn(x1s, mod_s, *ffn_w, nb=bs, tc=t_s, tf=SAMPLE_FFN_TF, sub=t_s, alpha=alpha)

        conv_p.append(cp)
        gla_p.append(sp)
        conv_s.append(cs)
        gla_s.append(ss)
    return (yp, ys, jnp.stack(conv_p), jnp.stack(gla_p), jnp.stack(conv_s), jnp.stack(gla_s))
```

```python
import functools

import jax
import jax.numpy as jnp
from jax import lax
from jax.experimental import pallas as pl
from jax.experimental.pallas import tpu as pltpu

F32 = jnp.float32
BF16 = jnp.bfloat16

CONV_WIDTH = 3
GLA_HEADS = 4
GATE_TAU = 16.0
LN_EPS = 1e-5
RMS_EPS = 1e-6
LANE = 128
SUBLANES = 8
BF16_ROWS = 16
VMEM_LIMIT = 60 * 1024 * 1024

PROMPT_MIXER = dict(nb=1, tc=256, chunk=64)
PROMPT_FFN = dict(nb=1, tc=1024, tf=512, sub=512)
MOD_TN = 1536
SAMPLE_MIXER_NB = 4
SAMPLE_FFN_TF = 512
LAST_PIECE = 256
GATE_LHS_TERM = (0, 1, 2, 0, 1, 0)
GATE_RHS_TERM = (0, 0, 0, 1, 1, 2)


def _bf16_terms(a):
    t1 = a.astype(BF16)
    r1 = a - t1.astype(F32)
    t2 = r1.astype(BF16)
    t3 = (r1 - t2.astype(F32)).astype(BF16)
    return t1, t2, t3


def _residual_layernorm(x, gate, branch, g, b, alpha):
    y = x + (gate * (1.0 / alpha)) * branch
    mu = jnp.mean(y, axis=-1, keepdims=True)
    d = y - mu
    var = jnp.mean(d * d, axis=-1, keepdims=True)
    return d * lax.rsqrt(var + LN_EPS / (alpha * alpha)) * g + b


def _silu(a):
    return a * jax.nn.sigmoid(a)


def _mod_kernel(c_ref, w_ref, b_ref, o_ref):
    a = _silu(c_ref[...]).astype(BF16)
    o_ref[...] = jnp.dot(a, w_ref[...].astype(BF16), preferred_element_type=F32) + b_ref[...]


def _modulation(c_all, w_ada, b_ada, *, tn):
    m, d = c_all.shape
    n = w_ada.shape[1]
    return pl.pallas_call(
        _mod_kernel,
        grid=(n // tn,),
        in_specs=[pl.BlockSpec((m, d), lambda j: (0, 0)),
                  pl.BlockSpec((d, tn), lambda j: (0, j)),
                  pl.BlockSpec((1, tn), lambda j: (0, j))],
        out_specs=pl.BlockSpec((m, tn), lambda j: (0, j)),
        out_shape=jax.ShapeDtypeStruct((m, n), F32),
        compiler_params=pltpu.CompilerParams(dimension_semantics=("arbitrary",),
                                             vmem_limit_bytes=VMEM_LIMIT),
        name="adaln_modulation",
    )(c_all, w_ada, b_ada.reshape(1, n))


def _mixer_kernel(x_ref, mod_ref, conv0_ref, s0_ref, w_in_ref, w_fl_ref, w_conv_ref, w_f_ref, b_f_ref,
                  gain_ref, w_o_ref, ln_g_ref, ln_b_ref,
                  x1_ref, conv_ref, s_ref,
                  h_s, hb_s, u_s, b_s, qk_s, dg_s, qs_s, kd_s, v_s, go_s, st_s, a_s, kv_s, stb_s, mix_s,
                  *, nb, tc, chunk, gate_rank, alpha, side_work=None):
    d_model = x_ref.shape[-1]
    d_conv = conv_ref.shape[-1]
    head_k, head_v = s_ref.shape[-2], s_ref.shape[-1]
    gla_dk = GLA_HEADS * head_k
    d_gla = GLA_HEADS * head_v
    rows = nb * tc
    n_chunks = tc // chunk
    tot_chunks = rows // chunk
    levels = [chunk >> i for i in range(1, chunk.bit_length())]
    contract0 = (((0,), (0,)), ((), ()))
    contract1 = (((1,), (1,)), ((), ()))

    @pl.when(pl.program_id(1) == 0)
    def _():
        conv_ref[...] = conv0_ref[...]
        for i in range(nb):
            for hd in range(GLA_HEADS):
                st_s[i, hd] = s0_ref[i, hd].T

    h = x_ref[...] * (1.0 + mod_ref[:, 1]) + mod_ref[:, 0]
    h_s[...] = h.reshape(rows, d_model).astype(BF16)

    def proj(lo, width):
        return jnp.dot(h_s[...], w_in_ref[:, lo:lo + width], preferred_element_type=F32)

    o_hb, o_hc, o_hin = 0, d_conv, 2 * d_conv
    o_q = 3 * d_conv
    o_k = o_q + gla_dk
    o_v = o_k + gla_dk
    o_go = o_v + d_gla

    flr = jnp.dot(h_s[...], w_fl_ref[...], preferred_element_type=F32)
    t1 = flr.astype(BF16).astype(F32)
    t2 = (flr - t1).astype(BF16).astype(F32)
    t3 = (flr - t1 - t2).astype(BF16).astype(F32)
    terms = (t1, t2, t3)
    grp = lax.broadcasted_iota(jnp.int32, flr.shape, 1) // gate_rank
    lhs = terms[0]
    for g, term in enumerate(GATE_LHS_TERM):
        if term:
            lhs = jnp.where(grp == g, terms[term], lhs)
    z = jnp.dot(lhs.astype(BF16), w_f_ref[...], preferred_element_type=F32) + b_f_ref[...]
    q = proj(o_q, gla_dk) * (head_k ** -0.5)
    k = proj(o_k, gla_dk)
    lf = jax.nn.log_sigmoid(z) * (1.0 / GATE_TAU)
    lf1 = lf.astype(BF16)
    rem = lf - lf1.astype(F32)
    lf2 = rem.astype(BF16)
    lf3 = (rem - lf2.astype(F32)).astype(BF16)
    ri = lax.broadcasted_iota(jnp.int32, (rows, rows), 0)
    ci = lax.broadcasted_iota(jnp.int32, (rows, rows), 1)
    ltri = ((ri // chunk == ci // chunk) & (ci <= ri)).astype(BF16)
    parts = jnp.dot(ltri, jnp.concatenate([lf1, lf2, lf3], axis=1), preferred_element_type=F32)
    v_s[...] = proj(o_v, d_gla).astype(BF16)
    b = parts[:, 0:gla_dk] + parts[:, gla_dk:2 * gla_dk] + parts[:, 2 * gla_dk:3 * gla_dk]

    def block_rows(block, r):
        picked = [b[n * block + r:n * block + r + 1, :] for n in range(rows // block)]
        spread = jnp.concatenate([jnp.broadcast_to(e, (block, gla_dk)) for e in picked], axis=0)
        return picked, spread

    b_end, b_tot = block_rows(chunk, chunk - 1)
    chunk_decay = [jnp.exp(e) for e in b_end]
    qs_s[...] = (q * jnp.exp(b)).astype(BF16)
    kd_s[...] = (k * jnp.exp(b_tot - b)).astype(BF16)
    for hd in range(GLA_HEADS):
        ks = slice(hd * head_k, (hd + 1) * head_k)
        dg_s[hd] = jnp.sum(q[:, ks] * k[:, ks], axis=-1, keepdims=True)

    n_lane_tiles = gla_dk // LANE
    for lt in range(n_lane_tiles):
        b_s[lt] = b[:, lt * LANE:(lt + 1) * LANE]
    row_id = lax.broadcasted_iota(jnp.int32, (rows, gla_dk), 0)
    groups = rows // SUBLANES
    sub_id = lax.broadcasted_iota(jnp.int32, (groups, SUBLANES, LANE), 1)

    def level_ref(s):
        if s >= SUBLANES:
            return block_rows(2 * s, s)[1]
        tiles = []
        for lt in range(n_lane_tiles):
            out = None
            for r0 in range(s, SUBLANES, 2 * s):
                pick = b_s[lt, pl.ds(r0, groups, stride=SUBLANES), :]
                cand = jnp.broadcast_to(pick[:, None, :], (groups, SUBLANES, LANE))
                out = cand if out is None else jnp.where(sub_id >= r0 - s, cand, out)
            tiles.append(out.reshape(rows, LANE))
        return jnp.concatenate(tiles, axis=1)

    for lev, s in enumerate(levels):
        upper = (row_id & (2 * s - 1)) >= s
        e = b - level_ref(s)
        decay = jnp.exp(jnp.minimum(jnp.where(upper, e, -e), 0.0))
        qk_s[lev] = (jnp.where(upper, q, k) * decay).astype(BF16)
    go_s[...] = proj(o_go, d_gla)

    cr = lax.broadcasted_iota(jnp.int32, (chunk, chunk), 0)
    cc = lax.broadcasted_iota(jnp.int32, (chunk, chunk), 1)
    diagonal = cr == cc
    level_masks = [((cr & -(2 * s)) == (cc & -(2 * s))) & ((cr & s) != 0) & ((cc & s) == 0)
                   for s in levels]
    gain = gain_ref[...]

    head_chunks = [(n, hd) for n in range(tot_chunks) for hd in range(GLA_HEADS)]

    def slices(n, hd):
        return (slice(n * chunk, (n + 1) * chunk), slice(hd * head_k, (hd + 1) * head_k),
                slice(hd * head_v, (hd + 1) * head_v))

    def gla_scores_and_updates():
        for n, hd in head_chunks:
            rs, ks, vs = slices(n, hd)
            a = jnp.where(diagonal, dg_s[hd, rs, :], 0.0)
            for lev, mask in enumerate(level_masks):
                f = qk_s[lev, rs, ks]
                p = lax.dot_general(f, f, contract1, preferred_element_type=F32)
                a = jnp.where(mask, p, a)
            a_s[hd, rs, :] = a.astype(BF16)
            kv_s[n, hd] = lax.dot_general(v_s[rs, vs], kd_s[rs, ks], contract0,
                                          preferred_element_type=F32)

    def gla_state_recurrence():
        for n, hd in head_chunks:
            _, ks, _ = slices(n, hd)
            st = st_s[n // n_chunks, hd]
            stb_s[n, hd] = st.T.astype(BF16)
            st_s[n // n_chunks, hd] = st * chunk_decay[n][:, ks] + kv_s[n, hd]

    def gla_outputs():
        for n, hd in head_chunks:
            rs, ks, vs = slices(n, hd)
            lhs = jnp.concatenate([qs_s[rs, ks], a_s[hd, rs, :]], axis=1)
            rhs = jnp.concatenate([stb_s[n, hd], v_s[rs, vs]], axis=0)
            o = jnp.dot(lhs, rhs, preferred_element_type=F32)
            o = o * lax.rsqrt(jnp.mean(o * o, axis=-1, keepdims=True) + RMS_EPS) * gain
            mix_s[rs, d_conv + hd * head_v:d_conv + (hd + 1) * head_v] = (
                o * _silu(go_s[rs, vs])).astype(BF16)

    row = lax.broadcasted_iota(jnp.int32, (tc, d_conv), 0)
    w0, w1, w2 = w_conv_ref[0:1, :], w_conv_ref[1:2, :], w_conv_ref[2:3, :]

    def conv_segment(i):
        rs = slice(i * tc, (i + 1) * tc)
        u = u_s[rs, :]
        st = conv_ref[i]
        p0, p1 = st[0:1, :], st[1:2, :]
        u1 = jnp.where(row == 0, p1, pltpu.roll(u, 1, axis=0))
        u2 = jnp.where(row == 0, p0, jnp.where(row == 1, p1, pltpu.roll(u, 2, axis=0)))
        conv = u2 * w0 + u1 * w1 + u * w2
        mix_s[rs, 0:d_conv] = (hb_s[rs, :] * conv).astype(BF16)
        conv_ref[i] = u[tc - (CONV_WIDTH - 1):tc, :]

    gla_scores_and_updates()
    if side_work is not None:
        side_work()
    u_s[...] = proj(o_hc, d_conv) * proj(o_hin, d_conv)
    gla_state_recurrence()
    hb_s[...] = proj(o_hb, d_conv)
    for i in range(nb):
        conv_segment(i)
    mo = jnp.dot(mix_s[:, 0:d_conv], w_o_ref[0:d_conv, :], preferred_element_type=F32)
    gla_outputs()
    half = rows // 2
    for r in (0, half):
        mo_r = mo[r:r + half] + jnp.dot(mix_s[r:r + half, d_conv:d_model],
                                        w_o_ref[d_conv:d_model, :], preferred_element_type=F32)
        if nb == 1:
            x1_ref[0, r:r + half, :] = _residual_layernorm(
                x_ref[0, r:r + half, :], mod_ref[0, 2], mo_r, ln_g_ref[...], ln_b_ref[...], alpha)
        else:
            bs_ = slice(r // tc, (r + half) // tc)
            x1_ref[bs_] = _residual_layernorm(
                x_ref[bs_], mod_ref[bs_, 2], mo_r.reshape(-1, tc, d_model),
                ln_g_ref[...], ln_b_ref[...], alpha)

    @pl.when(pl.program_id(1) == pl.num_programs(1) - 1)
    def _():
        for i in range(nb):
            for hd in range(GLA_HEADS):
                s_ref[i, hd] = st_s[i, hd].T


def _mixer_cast_kernel(*refs, n_in, n_cast, **static):
    ins, rest = refs[:n_in], refs[n_in:]
    cast_in, rest = rest[:n_cast], rest[n_cast:]
    outs, rest = rest[:3], rest[3:]
    cast_out, scratch = rest[:n_cast], rest[n_cast:]
    def cast_slices():
        for src, dst in zip(cast_in, cast_out):
            dst[...] = src[...].astype(dst.dtype)

    _mixer_kernel(*ins, *outs, *scratch, side_work=cast_slices, **static)


def _cast_block_rows(n_rows, n_steps):
    return next(m for m in range(BF16_ROWS, n_rows + 1, BF16_ROWS)
                if n_rows % m == 0 and m * n_steps >= n_rows)


def _mixer(x, mod, conv0, s0, w_in, w_fl, w_conv, w_f, b_f, gain, w_o, ln_g, ln_b,
           *, nb, tc, chunk, gate_rank, alpha, cast=()):
    bsz, t, d = x.shape
    d_conv = conv0.shape[-1]
    heads, head_k, head_v = s0.shape[1:]
    gla_dk, d_gla = heads * head_k, heads * head_v
    rows = nb * tc
    t_tiles = t // tc
    n_steps = (bsz // nb) * t_tiles
    const2 = lambda bi, ti: (0, 0)
    resident = functools.partial(pl.BlockSpec, index_map=const2, pipeline_mode=pl.Buffered(1))
    static = dict(nb=nb, tc=tc, chunk=chunk, gate_rank=gate_rank, alpha=alpha)
    kern = functools.partial(_mixer_cast_kernel, n_in=13, n_cast=len(cast), **static)

    def cast_spec(w):
        m = _cast_block_rows(w.shape[0], n_steps)
        last = w.shape[0] // m - 1
        return pl.BlockSpec((m, w.shape[1]),
                            lambda bi, ti: (jnp.minimum(bi * t_tiles + ti, last), 0))

    cast_specs = [cast_spec(w) for w in cast]
    return pl.pallas_call(
        kern,
        grid=(bsz // nb, t_tiles),
        in_specs=[
            pl.BlockSpec((nb, tc, d), lambda bi, ti: (bi, ti, 0)),
            pl.BlockSpec((nb, 6, 1, d), lambda bi, ti: (bi, 0, 0, 0)),
            pl.BlockSpec((nb, CONV_WIDTH - 1, d_conv), lambda bi, ti: (bi, 0, 0)),
            pl.BlockSpec((nb, heads, head_k, head_v), lambda bi, ti: (bi, 0, 0, 0)),
            resident(w_in.shape),
            resident(w_fl.shape),
            pl.BlockSpec(w_conv.shape, const2),
            pl.BlockSpec(w_f.shape, const2),
            pl.BlockSpec(b_f.shape, const2),
            pl.BlockSpec(gain.shape, const2),
            resident(w_o.shape),
            pl.BlockSpec(ln_g.shape, const2),
            pl.BlockSpec(ln_b.shape, const2),
        ] + cast_specs,
        out_specs=[
            pl.BlockSpec((nb, tc, d), lambda bi, ti: (bi, ti, 0)),
            pl.BlockSpec((nb, CONV_WIDTH - 1, d_conv), lambda bi, ti: (bi, 0, 0)),
            pl.BlockSpec((nb, heads, head_k, head_v), lambda bi, ti: (bi, 0, 0, 0)),
        ] + cast_specs,
        out_shape=[
            jax.ShapeDtypeStruct((bsz, t, d), F32),
            jax.ShapeDtypeStruct(conv0.shape, F32),
            jax.ShapeDtypeStruct(s0.shape, F32),
        ] + [jax.ShapeDtypeStruct(w.shape, BF16) for w in cast],
        scratch_shapes=[
            pltpu.VMEM((rows, d), BF16),
            pltpu.VMEM((rows, d_conv), F32),
            pltpu.VMEM((rows, d_conv), F32),
            pltpu.VMEM((gla_dk // LANE, rows, LANE), F32),
            pltpu.VMEM((chunk.bit_length() - 1, rows, gla_dk), BF16),
            pltpu.VMEM((heads, rows, 1), F32),
            pltpu.VMEM((rows, gla_dk), BF16),
            pltpu.VMEM((rows, gla_dk), BF16),
            pltpu.VMEM((rows, d_gla), BF16),
            pltpu.VMEM((rows, d_gla), F32),
            pltpu.VMEM((nb, heads, head_v, head_k), F32),
            pltpu.VMEM((heads, rows, chunk), BF16),
            pltpu.VMEM((rows // chunk, heads, head_v, head_k), F32),
            pltpu.VMEM((rows // chunk, heads, head_k, head_v), BF16),
            pltpu.VMEM((rows, d), BF16),
        ],
        compiler_params=pltpu.CompilerParams(dimension_semantics=("arbitrary", "arbitrary"),
                                             vmem_limit_bytes=VMEM_LIMIT),
        name="mixer",
    )(x, mod, conv0, s0, w_in, w_fl, w_conv, w_f, b_f, gain, w_o, ln_g, ln_b, *cast)


def _ffn_kernel(x1_ref, mod_ref, wg_hbm, wu_hbm, wd_hbm, ln_g_ref, ln_b_ref, o_ref,
                h2_s, wg_s, wu_s, wd_s, sems, *, nb, tc, tf, sub, alpha):
    d_model = x1_ref.shape[-1]
    n_f = wg_hbm.shape[1] // tf
    rows = nb * tc
    r = pl.program_id(0)
    first_slot = lax.rem(r * n_f, 2)

    def slot_of(f):
        return lax.rem(first_slot + f, 2)

    def weight_copies(f, slot):
        cols = pl.ds(f * tf if isinstance(f, int) else pl.multiple_of(f * tf, tf), tf)
        return (pltpu.make_async_copy(wg_hbm.at[:, cols], wg_s.at[slot], sems.at[0, slot]),
                pltpu.make_async_copy(wu_hbm.at[:, cols], wu_s.at[slot], sems.at[1, slot]),
                pltpu.make_async_copy(wd_hbm.at[cols, :], wd_s.at[slot], sems.at[2, slot]))

    def start(f, slot):
        for c in weight_copies(f, slot):
            c.start()

    def wait(f, slot):
        for c in weight_copies(f, slot):
            c.wait()

    if nb == 1:
        subs = [(slice(0, 1), slice(q, q + sub), slice(q, q + sub)) for q in range(0, tc, sub)]
    else:
        subs = [(slice(0, nb), slice(0, tc), slice(0, rows))]

    def swiglu(rows_flat, slot):
        h2 = h2_s[rows_flat, :]
        a = jnp.dot(h2, wg_s[slot], preferred_element_type=F32)
        b = jnp.dot(h2, wu_s[slot], preferred_element_type=F32)
        part = jnp.dot((_silu(a) * b).astype(BF16), wd_s[slot], preferred_element_type=F32)
        return part.reshape(-1, tc if nb > 1 else sub, d_model)

    @pl.when(r == 0)
    def _():
        start(0, slot_of(0))

    start(1, slot_of(1))
    wait(0, slot_of(0))
    for bs_, rs_, fs_ in subs:
        h2 = x1_ref[bs_, rs_, :] * (1.0 + mod_ref[bs_, 4]) + mod_ref[bs_, 3]
        h2_s[fs_, :] = h2.reshape(-1, d_model).astype(BF16)
        o_ref[bs_, rs_, :] = swiglu(fs_, slot_of(0))

    def middle(f, carry):
        start(f + 1, slot_of(f + 1))
        wait(f, slot_of(f))
        for bs_, rs_, fs_ in subs:
            o_ref[bs_, rs_, :] += swiglu(fs_, slot_of(f))
        return carry

    lax.fori_loop(1, n_f - 1, middle, 0)

    @pl.when(r + 1 < pl.num_programs(0))
    def _():
        start(0, slot_of(n_f))

    wait(n_f - 1, slot_of(n_f - 1))
    last_slot = slot_of(n_f - 1)
    for bs_, rs_, fs_ in subs:
        if nb > 1:
            acc = o_ref[bs_, rs_, :] + swiglu(fs_, last_slot)
            o_ref[bs_, rs_, :] = _residual_layernorm(
                x1_ref[bs_, rs_, :], mod_ref[bs_, 5], acc, ln_g_ref[...], ln_b_ref[...], alpha)
            continue
        h2 = h2_s[fs_, :]
        g = (_silu(jnp.dot(h2, wg_s[last_slot], preferred_element_type=F32))
             * jnp.dot(h2, wu_s[last_slot], preferred_element_type=F32)).astype(BF16)
        for q in range(0, sub, LAST_PIECE):
            rows_q = slice(rs_.start + q, rs_.start + q + LAST_PIECE)
            acc = o_ref[0, rows_q, :] + jnp.dot(g[q:q + LAST_PIECE], wd_s[last_slot],
                                                preferred_element_type=F32)
            o_ref[0, rows_q, :] = _residual_layernorm(
                x1_ref[0, rows_q, :], mod_ref[0, 5], acc, ln_g_ref[...], ln_b_ref[...], alpha)


def _ffn(x1, mod, w_gate, w_up, w_down, ln_g, ln_b, *, nb, tc, tf, sub, alpha):
    bsz, t, d = x1.shape
    d_ff = w_gate.shape[1]
    rows = nb * tc
    t_tiles = t // tc
    assert d_ff % tf == 0 and d_ff // tf >= 3, "first, middle and last d_ff tiles are peeled"
    kern = functools.partial(_ffn_kernel, nb=nb, tc=tc, tf=tf, sub=sub, alpha=alpha)
    x_map = lambda ri: (ri // t_tiles, ri % t_tiles, 0)
    hbm = pl.BlockSpec(memory_space=pl.ANY)
    return pl.pallas_call(
        kern,
        grid=((bsz // nb) * t_tiles,),
        in_specs=[
            pl.BlockSpec((nb, tc, d), x_map),
            pl.BlockSpec((nb, 6, 1, d), lambda ri: (ri // t_tiles, 0, 0, 0)),
            hbm, hbm, hbm,
            pl.BlockSpec(ln_g.shape, lambda ri: (0, 0)),
            pl.BlockSpec(ln_b.shape, lambda ri: (0, 0)),
        ],
        out_specs=pl.BlockSpec((nb, tc, d), x_map),
        out_shape=jax.ShapeDtypeStruct((bsz, t, d), F32),
        scratch_shapes=[
            pltpu.VMEM((rows, d), BF16),
            pltpu.VMEM((2, d, tf), BF16),
            pltpu.VMEM((2, d, tf), BF16),
            pltpu.VMEM((2, tf, d), BF16),
            pltpu.SemaphoreType.DMA((3, 2)),
        ],
        compiler_params=pltpu.CompilerParams(dimension_semantics=("arbitrary",),
                                             vmem_limit_bytes=VMEM_LIMIT),
        name="ffn",
    )(x1, mod, w_gate, w_up, w_down, ln_g, ln_b)


def kernel(x_prompt, x_sample, c_prompt, c_sample, cache_conv, state_gla, w_ada, b_ada, w_in,
           w_conv, w_f, b_f, gla_gain, w_o, ln1_g, ln1_b, w_gate, w_up, w_down, ln2_g, ln2_b):
    depth = w_ada.shape[0]
    bp, _, d_model = x_prompt.shape
    bs, t_s, _ = x_sample.shape
    d_conv = cache_conv.shape[-1]
    heads, head_k, head_v = state_gla.shape[2:]
    gate_rank = w_f.shape[1]
    alpha = (2.0 * depth) ** 0.25

    yp, ys = x_prompt, x_sample
    conv_p, gla_p, conv_s, gla_s = [], [], [], []
    for l in range(depth):
        c_all = jnp.concatenate([c_prompt, c_sample], axis=0)
        pad_rows = -c_all.shape[0] % 8
        c_all = jnp.pad(c_all, ((0, pad_rows), (0, 0)))
        mod = _modulation(c_all, w_ada[l], b_ada[l], tn=MOD_TN)
        mod_p = mod[:bp].reshape(bp, 6, 1, d_model)
        mod_s = mod[bp:bp + bs].reshape(bs, 6, 1, d_model)

        n_main = w_in.shape[-1] - gate_rank
        copies = len(GATE_LHS_TERM)
        fill = LANE - copies * gate_rank
        w_in_b = w_in[l].astype(BF16)
        w_fl_b = jnp.concatenate([w_in[l][:, n_main:]] * copies
                                 + [jnp.zeros((d_model, fill), F32)], axis=1).astype(BF16)
        wf_terms = _bf16_terms(w_f[l])
        w_f_p = jnp.concatenate([wf_terms[t] for t in GATE_RHS_TERM]
                                + [jnp.zeros((fill, w_f.shape[-1]), BF16)], axis=0)
        w_o_b = w_o[l].astype(BF16)
        row = lambda a: a.reshape(1, -1)
        mixer_w = (w_in_b, w_fl_b, w_conv[l], w_f_p, row(b_f[l]), row(gla_gain[l]), w_o_b,
                   row(ln1_g[l]), row(ln1_b[l]))

        conv0 = jnp.zeros((bp, CONV_WIDTH - 1, d_conv), F32)
        gla0 = jnp.zeros((bp, heads, head_k, head_v), F32)
        x1p, cp, sp, wg_b, wu_b, wd_b = _mixer(
            yp, mod_p, conv0, gla0, *mixer_w, gate_rank=gate_rank, alpha=alpha,
            cast=(w_gate[l], w_up[l], w_down[l]), **PROMPT_MIXER)
        ffn_w = (wg_b, wu_b, wd_b, row(ln2_g[l]), row(ln2_b[l]))
        yp = _ffn(x1p, mod_p, *ffn_w, alpha=alpha, **PROMPT_FFN)

        x1s, cs, ss = _mixer(ys, mod_s, cache_conv[l], state_gla[l], *mixer_w,
                             nb=SAMPLE_MIXER_NB, tc=t_s, chunk=t_s, gate_rank=gate_rank,
                             alpha=alpha)
        ys = _ffn(x1s, mod_s, *ffn_w, nb=bs, tc=t_s, tf=SAMPLE_FFN_TF, sub=t_s, alpha=alpha)

        conv_p.append(cp)
        gla_p.append(sp)
        conv_s.append(cs)
        gla_s.append(ss)
    return (yp, ys, jnp.stack(conv_p), jnp.stack(gla_p), jnp.stack(conv_s), jnp.stack(gla_s))
```

```python
import functools

import jax
import jax.numpy as jnp
from jax import lax
from jax.experimental import pallas as pl
from jax.experimental.pallas import tpu as pltpu

F32 = jnp.float32
BF16 = jnp.bfloat16

CONV_WIDTH = 3
GLA_HEADS = 4
GATE_TAU = 16.0
LN_EPS = 1e-5
RMS_EPS = 1e-6
LANE = 128
SUBLANES = 8
BF16_ROWS = 16
VMEM_LIMIT = 60 * 1024 * 1024

PROMPT_MIXER = dict(nb=1, tc=256, chunk=64)
PROMPT_FFN = dict(nb=1, tc=1024, tf=512, sub=512)
MOD_TN = 1536
SAMPLE_MIXER_NB = 4
SAMPLE_FFN_TF = 512
GATE_LHS_TERM = (0, 1, 2, 0, 1, 0)
GATE_RHS_TERM = (0, 0, 0, 1, 1, 2)


def _bf16_terms(a):
    t1 = a.astype(BF16)
    r1 = a - t1.astype(F32)
    t2 = r1.astype(BF16)
    t3 = (r1 - t2.astype(F32)).astype(BF16)
    return t1, t2, t3


def _residual_layernorm(x, gate, branch, g, b, alpha):
    y = x + (gate * (1.0 / alpha)) * branch
    mu = jnp.mean(y, axis=-1, keepdims=True)
    d = y - mu
    var = jnp.mean(d * d, axis=-1, keepdims=True)
    return d * lax.rsqrt(var + LN_EPS / (alpha * alpha)) * g + b


def _silu(a):
    return a * jax.nn.sigmoid(a)


def _mod_kernel(c_ref, w_ref, b_ref, o_ref):
    a = _silu(c_ref[...]).astype(BF16)
    o_ref[...] = jnp.dot(a, w_ref[...].astype(BF16), preferred_element_type=F32) + b_ref[...]


def _modulation(c_all, w_ada, b_ada, *, tn):
    m, d = c_all.shape
    n = w_ada.shape[1]
    return pl.pallas_call(
        _mod_kernel,
        grid=(n // tn,),
        in_specs=[pl.BlockSpec((m, d), lambda j: (0, 0)),
                  pl.BlockSpec((d, tn), lambda j: (0, j)),
                  pl.BlockSpec((1, tn), lambda j: (0, j))],
        out_specs=pl.BlockSpec((m, tn), lambda j: (0, j)),
        out_shape=jax.ShapeDtypeStruct((m, n), F32),
        compiler_params=pltpu.CompilerParams(dimension_semantics=("arbitrary",),
                                             vmem_limit_bytes=VMEM_LIMIT),
        name="adaln_modulation",
    )(c_all, w_ada, b_ada.reshape(1, n))


def _mixer_kernel(x_ref, mod_ref, conv0_ref, s0_ref, w_in_ref, w_fl_ref, w_conv_ref, w_f_ref, b_f_ref,
                  gain_ref, w_o_ref, ln_g_ref, ln_b_ref,
                  x1_ref, conv_ref, s_ref,
                  h_s, hb_s, u_s, b_s, qk_s, dg_s, qs_s, kd_s, v_s, go_s, st_s, a_s, kv_s, stb_s, mix_s,
                  *, nb, tc, chunk, gate_rank, alpha, side_work=None):
    d_model = x_ref.shape[-1]
    d_conv = conv_ref.shape[-1]
    head_k, head_v = s_ref.shape[-2], s_ref.shape[-1]
    gla_dk = GLA_HEADS * head_k
    d_gla = GLA_HEADS * head_v
    rows = nb * tc
    n_chunks = tc // chunk
    tot_chunks = rows // chunk
    levels = [chunk >> i for i in range(1, chunk.bit_length())]
    contract0 = (((0,), (0,)), ((), ()))
    contract1 = (((1,), (1,)), ((), ()))

    @pl.when(pl.program_id(1) == 0)
    def _():
        conv_ref[...] = conv0_ref[...]
        for i in range(nb):
            for hd in range(GLA_HEADS):
                st_s[i, hd] = s0_ref[i, hd].T

    h = x_ref[...] * (1.0 + mod_ref[:, 1]) + mod_ref[:, 0]
    h_s[...] = h.reshape(rows, d_model).astype(BF16)

    def proj(lo, width):
        return jnp.dot(h_s[...], w_in_ref[:, lo:lo + width], preferred_element_type=F32)

    o_hb, o_hc, o_hin = 0, d_conv, 2 * d_conv
    o_q = 3 * d_conv
    o_k = o_q + gla_dk
    o_v = o_k + gla_dk
    o_go = o_v + d_gla

    flr = jnp.dot(h_s[...], w_fl_ref[...], preferred_element_type=F32)
    t1 = flr.astype(BF16).astype(F32)
    t2 = (flr - t1).astype(BF16).astype(F32)
    t3 = (flr - t1 - t2).astype(BF16).astype(F32)
    terms = (t1, t2, t3)
    grp = lax.broadcasted_iota(jnp.int32, flr.shape, 1) // gate_rank
    lhs = terms[0]
    for g, term in enumerate(GATE_LHS_TERM):
        if term:
            lhs = jnp.where(grp == g, terms[term], lhs)
    z = jnp.dot(lhs.astype(BF16), w_f_ref[...], preferred_element_type=F32) + b_f_ref[...]
    q = proj(o_q, gla_dk) * (head_k ** -0.5)
    k = proj(o_k, gla_dk)
    lf = jax.nn.log_sigmoid(z) * (1.0 / GATE_TAU)
    lf1 = lf.astype(BF16)
    rem = lf - lf1.astype(F32)
    lf2 = rem.astype(BF16)
    lf3 = (rem - lf2.astype(F32)).astype(BF16)
    ri = lax.broadcasted_iota(jnp.int32, (rows, rows), 0)
    ci = lax.broadcasted_iota(jnp.int32, (rows, rows), 1)
    ltri = ((ri // chunk == ci // chunk) & (ci <= ri)).astype(BF16)
    parts = jnp.dot(ltri, jnp.concatenate([lf1, lf2, lf3], axis=1), preferred_element_type=F32)
    v_s[...] = proj(o_v, d_gla).astype(BF16)
    b = parts[:, 0:gla_dk] + parts[:, gla_dk:2 * gla_dk] + parts[:, 2 * gla_dk:3 * gla_dk]

    def block_rows(block, r):
        picked = [b[n * block + r:n * block + r + 1, :] for n in range(rows // block)]
        spread = jnp.concatenate([jnp.broadcast_to(e, (block, gla_dk)) for e in picked], axis=0)
        return picked, spread

    b_end, b_tot = block_rows(chunk, chunk - 1)
    chunk_decay = [jnp.exp(e) for e in b_end]
    qs_s[...] = (q * jnp.exp(b)).astype(BF16)
    kd_s[...] = (k * jnp.exp(b_tot - b)).astype(BF16)
    for hd in range(GLA_HEADS):
        ks = slice(hd * head_k, (hd + 1) * head_k)
        dg_s[hd] = jnp.sum(q[:, ks] * k[:, ks], axis=-1, keepdims=True)

    n_lane_tiles = gla_dk // LANE
    for lt in range(n_lane_tiles):
        b_s[lt] = b[:, lt * LANE:(lt + 1) * LANE]
    row_id = lax.broadcasted_iota(jnp.int32, (rows, gla_dk), 0)
    groups = rows // SUBLANES
    sub_id = lax.broadcasted_iota(jnp.int32, (groups, SUBLANES, LANE), 1)

    def level_ref(s):
        if s >= SUBLANES:
            return block_rows(2 * s, s)[1]
        tiles = []
        for lt in range(n_lane_tiles):
            out = None
            for r0 in range(s, SUBLANES, 2 * s):
                pick = b_s[lt, pl.ds(r0, groups, stride=SUBLANES), :]
                cand = jnp.broadcast_to(pick[:, None, :], (groups, SUBLANES, LANE))
                out = cand if out is None else jnp.where(sub_id >= r0 - s, cand, out)
            tiles.append(out.reshape(rows, LANE))
        return jnp.concatenate(tiles, axis=1)

    for lev, s in enumerate(levels):
        upper = (row_id & (2 * s - 1)) >= s
        e = b - level_ref(s)
        decay = jnp.exp(jnp.minimum(jnp.where(upper, e, -e), 0.0))
        qk_s[lev] = (jnp.where(upper, q, k) * decay).astype(BF16)
    go_s[...] = proj(o_go, d_gla)

    cr = lax.broadcasted_iota(jnp.int32, (chunk, chunk), 0)
    cc = lax.broadcasted_iota(jnp.int32, (chunk, chunk), 1)
    diagonal = cr == cc
    level_masks = [((cr & -(2 * s)) == (cc & -(2 * s))) & ((cr & s) != 0) & ((cc & s) == 0)
                   for s in levels]
    gain = gain_ref[...]

    head_chunks = [(n, hd) for n in range(tot_chunks) for hd in range(GLA_HEADS)]

    def slices(n, hd):
        return (slice(n * chunk, (n + 1) * chunk), slice(hd * head_k, (hd + 1) * head_k),
                slice(hd * head_v, (hd + 1) * head_v))

    def gla_scores_and_updates():
        for n, hd in head_chunks:
            rs, ks, vs = slices(n, hd)
            a = jnp.where(diagonal, dg_s[hd, rs, :], 0.0)
            for lev, mask in enumerate(level_masks):
                f = qk_s[lev, rs, ks]
                p = lax.dot_general(f, f, contract1, preferred_element_type=F32)
                a = jnp.where(mask, p, a)
            a_s[hd, rs, :] = a.astype(BF16)
            kv_s[n, hd] = lax.dot_general(v_s[rs, vs], kd_s[rs, ks], contract0,
                                          preferred_element_type=F32)

    def gla_state_recurrence():
        for n, hd in head_chunks:
            _, ks, _ = slices(n, hd)
            st = st_s[n // n_chunks, hd]
            stb_s[n, hd] = st.T.astype(BF16)
            st_s[n // n_chunks, hd] = st * chunk_decay[n][:, ks] + kv_s[n, hd]

    def gla_outputs():
        for n, hd in head_chunks:
            rs, ks, vs = slices(n, hd)
            lhs = jnp.concatenate([qs_s[rs, ks], a_s[hd, rs, :]], axis=1)
            rhs = jnp.concatenate([stb_s[n, hd], v_s[rs, vs]], axis=0)
            o = jnp.dot(lhs, rhs, preferred_element_type=F32)
            o = o * lax.rsqrt(jnp.mean(o * o, axis=-1, keepdims=True) + RMS_EPS) * gain
            mix_s[rs, d_conv + hd * head_v:d_conv + (hd + 1) * head_v] = (
                o * _silu(go_s[rs, vs])).astype(BF16)

    row = lax.broadcasted_iota(jnp.int32, (tc, d_conv), 0)
    w0, w1, w2 = w_conv_ref[0:1, :], w_conv_ref[1:2, :], w_conv_ref[2:3, :]

    def conv_segment(i):
        rs = slice(i * tc, (i + 1) * tc)
        u = u_s[rs, :]
        st = conv_ref[i]
        p0, p1 = st[0:1, :], st[1:2, :]
        u1 = jnp.where(row == 0, p1, pltpu.roll(u, 1, axis=0))
        u2 = jnp.where(row == 0, p0, jnp.where(row == 1, p1, pltpu.roll(u, 2, axis=0)))
        conv = u2 * w0 + u1 * w1 + u * w2
        mix_s[rs, 0:d_conv] = (hb_s[rs, :] * conv).astype(BF16)
        conv_ref[i] = u[tc - (CONV_WIDTH - 1):tc, :]

    gla_scores_and_updates()
    if side_work is not None:
        side_work()
    u_s[...] = proj(o_hc, d_conv) * proj(o_hin, d_conv)
    gla_state_recurrence()
    hb_s[...] = proj(o_hb, d_conv)
    for i in range(nb):
        conv_segment(i)
    mo = jnp.dot(mix_s[:, 0:d_conv], w_o_ref[0:d_conv, :], preferred_element_type=F32)
    gla_outputs()
    half = rows // 2
    for r in (0, half):
        mo_r = mo[r:r + half] + jnp.dot(mix_s[r:r + half, d_conv:d_model],
                                        w_o_ref[d_conv:d_model, :], preferred_element_type=F32)
        if nb == 1:
            x1_ref[0, r:r + half, :] = _residual_layernorm(
                x_ref[0, r:r + half, :], mod_ref[0, 2], mo_r, ln_g_ref[...], ln_b_ref[...], alpha)
        else:
            bs_ = slice(r // tc, (r + half) // tc)
            x1_ref[bs_] = _residual_layernorm(
                x_ref[bs_], mod_ref[bs_, 2], mo_r.reshape(-1, tc, d_model),
                ln_g_ref[...], ln_b_ref[...], alpha)

    @pl.when(pl.program_id(1) == pl.num_programs(1) - 1)
    def _():
        for i in range(nb):
            for hd in range(GLA_HEADS):
                s_ref[i, hd] = st_s[i, hd].T


def _mixer_cast_kernel(*refs, n_in, n_cast, **static):
    ins, rest = refs[:n_in], refs[n_in:]
    cast_in, rest = rest[:n_cast], rest[n_cast:]
    outs, rest = rest[:3], rest[3:]
    cast_out, scratch = rest[:n_cast], rest[n_cast:]
    def cast_slices():
        for src, dst in zip(cast_in, cast_out):
            dst[...] = src[...].astype(dst.dtype)

    _mixer_kernel(*ins, *outs, *scratch, side_work=cast_slices, **static)


def _cast_block_rows(n_rows, n_steps):
    return next(m for m in range(BF16_ROWS, n_rows + 1, BF16_ROWS)
                if n_rows % m == 0 and m * n_steps >= n_rows)


def _mixer(x, mod, conv0, s0, w_in, w_fl, w_conv, w_f, b_f, gain, w_o, ln_g, ln_b,
           *, nb, tc, chunk, gate_rank, alpha, cast=()):
    bsz, t, d = x.shape
    d_conv = conv0.shape[-1]
    heads, head_k, head_v = s0.shape[1:]
    gla_dk, d_gla = heads * head_k, heads * head_v
    rows = nb * tc
    t_tiles = t // tc
    n_steps = (bsz // nb) * t_tiles
    const2 = lambda bi, ti: (0, 0)
    resident = functools.partial(pl.BlockSpec, index_map=const2, pipeline_mode=pl.Buffered(1))
    static = dict(nb=nb, tc=tc, chunk=chunk, gate_rank=gate_rank, alpha=alpha)
    kern = functools.partial(_mixer_cast_kernel, n_in=13, n_cast=len(cast), **static)

    def cast_spec(w):
        m = _cast_block_rows(w.shape[0], n_steps)
        last = w.shape[0] // m - 1
        return pl.BlockSpec((m, w.shape[1]),
                            lambda bi, ti: (jnp.minimum(bi * t_tiles + ti, last), 0))

    cast_specs = [cast_spec(w) for w in cast]
    return pl.pallas_call(
        kern,
        grid=(bsz // nb, t_tiles),
        in_specs=[
            pl.BlockSpec((nb, tc, d), lambda bi, ti: (bi, ti, 0)),
            pl.BlockSpec((nb, 6, 1, d), lambda bi, ti: (bi, 0, 0, 0)),
            pl.BlockSpec((nb, CONV_WIDTH - 1, d_conv), lambda bi, ti: (bi, 0, 0)),
            pl.BlockSpec((nb, heads, head_k, head_v), lambda bi, ti: (bi, 0, 0, 0)),
            resident(w_in.shape),
            resident(w_fl.shape),
            pl.BlockSpec(w_conv.shape, const2),
            pl.BlockSpec(w_f.shape, const2),
            pl.BlockSpec(b_f.shape, const2),
            pl.BlockSpec(gain.shape, const2),
            resident(w_o.shape),
            pl.BlockSpec(ln_g.shape, const2),
            pl.BlockSpec(ln_b.shape, const2),
        ] + cast_specs,
        out_specs=[
            pl.BlockSpec((nb, tc, d), lambda bi, ti: (bi, ti, 0)),
            pl.BlockSpec((nb, CONV_WIDTH - 1, d_conv), lambda bi, ti: (bi, 0, 0)),
            pl.BlockSpec((nb, heads, head_k, head_v), lambda bi, ti: (bi, 0, 0, 0)),
        ] + cast_specs,
        out_shape=[
            jax.ShapeDtypeStruct((bsz, t, d), F32),
            jax.ShapeDtypeStruct(conv0.shape, F32),
            jax.ShapeDtypeStruct(s0.shape, F32),
        ] + [jax.ShapeDtypeStruct(w.shape, BF16) for w in cast],
        scratch_shapes=[
            pltpu.VMEM((rows, d), BF16),
            pltpu.VMEM((rows, d_conv), F32),
            pltpu.VMEM((rows, d_conv), F32),
            pltpu.VMEM((gla_dk // LANE, rows, LANE), F32),
            pltpu.VMEM((chunk.bit_length() - 1, rows, gla_dk), BF16),
            pltpu.VMEM((heads, rows, 1), F32),
            pltpu.VMEM((rows, gla_dk), BF16),
            pltpu.VMEM((rows, gla_dk), BF16),
            pltpu.VMEM((rows, d_gla), BF16),
            pltpu.VMEM((rows, d_gla), F32),
            pltpu.VMEM((nb, heads, head_v, head_k), F32),
            pltpu.VMEM((heads, rows, chunk), BF16),
            pltpu.VMEM((rows // chunk, heads, head_v, head_k), F32),
            pltpu.VMEM((rows // chunk, heads, head_k, head_v), BF16),
            pltpu.VMEM((rows, d), BF16),
        ],
        compiler_params=pltpu.CompilerParams(dimension_semantics=("arbitrary", "arbitrary"),
                                             vmem_limit_bytes=VMEM_LIMIT),
        name="mixer",
    )(x, mod, conv0, s0, w_in, w_fl, w_conv, w_f, b_f, gain, w_o, ln_g, ln_b, *cast)


def _ffn_kernel(x1_ref, mod_ref, wg_hbm, wu_hbm, wd_hbm, ln_g_ref, ln_b_ref, o_ref,
                h2_s, wg_s, wu_s, wd_s, sems, *, nb, tc, tf, sub, alpha):
    d_model = x1_ref.shape[-1]
    n_f = wg_hbm.shape[1] // tf
    rows = nb * tc
    r = pl.program_id(0)
    first_slot = lax.rem(r * n_f, 2)

    def slot_of(f):
        return lax.rem(first_slot + f, 2)

    def weight_copies(f, slot):
        cols = pl.ds(f * tf if isinstance(f, int) else pl.multiple_of(f * tf, tf), tf)
        return (pltpu.make_async_copy(wg_hbm.at[:, cols], wg_s.at[slot], sems.at[0, slot]),
                pltpu.make_async_copy(wu_hbm.at[:, cols], wu_s.at[slot], sems.at[1, slot]),
                pltpu.make_async_copy(wd_hbm.at[cols, :], wd_s.at[slot], sems.at[2, slot]))

    def start(f, slot):
        for c in weight_copies(f, slot):
            c.start()

    def wait(f, slot):
        for c in weight_copies(f, slot):
            c.wait()

    if nb == 1:
        subs = [(slice(0, 1), slice(q, q + sub), slice(q, q + sub)) for q in range(0, tc, sub)]
    else:
        subs = [(slice(0, nb), slice(0, tc), slice(0, rows))]

    def swiglu(rows_flat, slot):
        h2 = h2_s[rows_flat, :]
        a = jnp.dot(h2, wg_s[slot], preferred_element_type=F32)
        b = jnp.dot(h2, wu_s[slot], preferred_element_type=F32)
        part = jnp.dot((_silu(a) * b).astype(BF16), wd_s[slot], preferred_element_type=F32)
        return part.reshape(-1, tc if nb > 1 else sub, d_model)

    @pl.when(r == 0)
    def _():
        start(0, slot_of(0))

    start(1, slot_of(1))
    wait(0, slot_of(0))
    for bs_, rs_, fs_ in subs:
        h2 = x1_ref[bs_, rs_, :] * (1.0 + mod_ref[bs_, 4]) + mod_ref[bs_, 3]
        h2_s[fs_, :] = h2.reshape(-1, d_model).astype(BF16)
        o_ref[bs_, rs_, :] = swiglu(fs_, slot_of(0))

    def middle(f, carry):
        start(f + 1, slot_of(f + 1))
        wait(f, slot_of(f))
        for bs_, rs_, fs_ in subs:
            o_ref[bs_, rs_, :] += swiglu(fs_, slot_of(f))
        return carry

    lax.fori_loop(1, n_f - 1, middle, 0)

    @pl.when(r + 1 < pl.num_programs(0))
    def _():
        start(0, slot_of(n_f))

    wait(n_f - 1, slot_of(n_f - 1))
    for bs_, rs_, fs_ in subs:
        acc = o_ref[bs_, rs_, :] + swiglu(fs_, slot_of(n_f - 1))
        o_ref[bs_, rs_, :] = _residual_layernorm(
            x1_ref[bs_, rs_, :], mod_ref[bs_, 5], acc, ln_g_ref[...], ln_b_ref[...], alpha)


def _ffn(x1, mod, w_gate, w_up, w_down, ln_g, ln_b, *, nb, tc, tf, sub, alpha):
    bsz, t, d = x1.shape
    d_ff = w_gate.shape[1]
    rows = nb * tc
    t_tiles = t // tc
    assert d_ff % tf == 0 and d_ff // tf >= 3, "first, middle and last d_ff tiles are peeled"
    kern = functools.partial(_ffn_kernel, nb=nb, tc=tc, tf=tf, sub=sub, alpha=alpha)
    x_map = lambda ri: (ri // t_tiles, ri % t_tiles, 0)
    hbm = pl.BlockSpec(memory_space=pl.ANY)
    return pl.pallas_call(
        kern,
        grid=((bsz // nb) * t_tiles,),
        in_specs=[
            pl.BlockSpec((nb, tc, d), x_map),
            pl.BlockSpec((nb, 6, 1, d), lambda ri: (ri // t_tiles, 0, 0, 0)),
            hbm, hbm, hbm,
            pl.BlockSpec(ln_g.shape, lambda ri: (0, 0)),
            pl.BlockSpec(ln_b.shape, lambda ri: (0, 0)),
        ],
        out_specs=pl.BlockSpec((nb, tc, d), x_map),
        out_shape=jax.ShapeDtypeStruct((bsz, t, d), F32),
        scratch_shapes=[
            pltpu.VMEM((rows, d), BF16),
            pltpu.VMEM((2, d, tf), BF16),
            pltpu.VMEM((2, d, tf), BF16),
            pltpu.VMEM((2, tf, d), BF16),
            pltpu.SemaphoreType.DMA((3, 2)),
        ],
        compiler_params=pltpu.CompilerParams(dimension_semantics=("arbitrary",),
                                             vmem_limit_bytes=VMEM_LIMIT),
        name="ffn",
    )(x1, mod, w_gate, w_up, w_down, ln_g, ln_b)


def kernel(x_prompt, x_sample, c_prompt, c_sample, cache_conv, state_gla, w_ada, b_ada, w_in,
           w_conv, w_f, b_f, gla_gain, w_o, ln1_g, ln1_b, w_gate, w_up, w_down, ln2_g, ln2_b):
    depth = w_ada.shape[0]
    bp, _, d_model = x_prompt.shape
    bs, t_s, _ = x_sample.shape
    d_conv = cache_conv.shape[-1]
    heads, head_k, head_v = state_gla.shape[2:]
    gate_rank = w_f.shape[1]
    alpha = (2.0 * depth) ** 0.25

    yp, ys = x_prompt, x_sample
    conv_p, gla_p, conv_s, gla_s = [], [], [], []
    for l in range(depth):
        c_all = jnp.concatenate([c_prompt, c_sample], axis=0)
        pad_rows = -c_all.shape[0] % 8
        c_all = jnp.pad(c_all, ((0, pad_rows), (0, 0)))
        mod = _modulation(c_all, w_ada[l], b_ada[l], tn=MOD_TN)
        mod_p = mod[:bp].reshape(bp, 6, 1, d_model)
        mod_s = mod[bp:bp + bs].reshape(bs, 6, 1, d_model)

        n_main = w_in.shape[-1] - gate_rank
        copies = len(GATE_LHS_TERM)
        fill = LANE - copies * gate_rank
        w_in_b = w_in[l].astype(BF16)
        w_fl_b = jnp.concatenate([w_in[l][:, n_main:]] * copies
                                 + [jnp.zeros((d_model, fill), F32)], axis=1).astype(BF16)
        wf_terms = _bf16_terms(w_f[l])
        w_f_p = jnp.concatenate([wf_terms[t] for t in GATE_RHS_TERM]
                                + [jnp.zeros((fill, w_f.shape[-1]), BF16)], axis=0)
        w_o_b = w_o[l].astype(BF16)
        row = lambda a: a.reshape(1, -1)
        mixer_w = (w_in_b, w_fl_b, w_conv[l], w_f_p, row(b_f[l]), row(gla_gain[l]), w_o_b,
                   row(ln1_g[l]), row(ln1_b[l]))

        conv0 = jnp.zeros((bp, CONV_WIDTH - 1, d_conv), F32)
        gla0 = jnp.zeros((bp, heads, head_k, head_v), F32)
        x1p, cp, sp, wg_b, wu_b, wd_b = _mixer(
            yp, mod_p, conv0, gla0, *mixer_w, gate_rank=gate_rank, alpha=alpha,
            cast=(w_gate[l], w_up[l], w_down[l]), **PROMPT_MIXER)
        ffn_w = (wg_b, wu_b, wd_b, row(ln2_g[l]), row(ln2_b[l]))
        yp = _ffn(x1p, mod_p, *ffn_w, alpha=alpha, **PROMPT_FFN)

        x1s, cs, ss = _mixer(ys, mod_s, cache_conv[l], state_gla[l], *mixer_w,
                             nb=SAMPLE_MIXER_NB, tc=t_s, chunk=t_s, gate_rank=gate_rank,
                             alpha=alpha)
        ys = _ffn(x1s, mod_s, *ffn_w, nb=bs, tc=t_s, tf=SAMPLE_FFN_TF, sub=t_s, alpha=alpha)

        conv_p.append(cp)
        gla_p.append(sp)
        conv_s.append(cs)
        gla_s.append(ss)
    return (yp, ys, jnp.stack(conv_p), jnp.stack(gla_p), jnp.stack(conv_s), jnp.stack(gla_s))
```

```python
import functools

import jax
import jax.numpy as jnp
from jax import lax
from jax.experimental import pallas as pl
from jax.experimental.pallas import tpu as pltpu

F32 = jnp.float32
BF16 = jnp.bfloat16

CONV_WIDTH = 3
GLA_HEADS = 4
GATE_TAU = 16.0
LN_EPS = 1e-5
RMS_EPS = 1e-6
LANE = 128
SUBLANES = 8
BF16_ROWS = 16
VMEM_LIMIT = 60 * 1024 * 1024

PROMPT_MIXER = dict(nb=1, tc=256, chunk=64)
PROMPT_FFN = dict(nb=1, tc=1024, tf=512, sub=512)
MOD_TN = 1536
SAMPLE_MIXER_NB = 4
SAMPLE_FFN_TF = 512
GATE_LHS_TERM = (0, 1, 2, 0, 1, 0)
GATE_RHS_TERM = (0, 0, 0, 1, 1, 2)


def _bf16_terms(a):
    t1 = a.astype(BF16)
    r1 = a - t1.astype(F32)
    t2 = r1.astype(BF16)
    t3 = (r1 - t2.astype(F32)).astype(BF16)
    return t1, t2, t3


def _residual_layernorm(x, gate, branch, g, b, alpha):
    y = x + (gate * (1.0 / alpha)) * branch
    mu = jnp.mean(y, axis=-1, keepdims=True)
    d = y - mu
    var = jnp.mean(d * d, axis=-1, keepdims=True)
    return d * lax.rsqrt(var + LN_EPS / (alpha * alpha)) * g + b


def _silu(a):
    return a * jax.nn.sigmoid(a)


def _mod_kernel(c_ref, w_ref, b_ref, o_ref):
    a = _silu(c_ref[...]).astype(BF16)
    o_ref[...] = jnp.dot(a, w_ref[...].astype(BF16), preferred_element_type=F32) + b_ref[...]


def _modulation(c_all, w_ada, b_ada, *, tn):
    m, d = c_all.shape
    n = w_ada.shape[1]
    return pl.pallas_call(
        _mod_kernel,
        grid=(n // tn,),
        in_specs=[pl.BlockSpec((m, d), lambda j: (0, 0)),
                  pl.BlockSpec((d, tn), lambda j: (0, j)),
                  pl.BlockSpec((1, tn), lambda j: (0, j))],
        out_specs=pl.BlockSpec((m, tn), lambda j: (0, j)),
        out_shape=jax.ShapeDtypeStruct((m, n), F32),
        compiler_params=pltpu.CompilerParams(dimension_semantics=("arbitrary",),
                                             vmem_limit_bytes=VMEM_LIMIT),
        name="adaln_modulation",
    )(c_all, w_ada, b_ada.reshape(1, n))


def _mixer_kernel(x_ref, mod_ref, conv0_ref, s0_ref, w_in_ref, w_fl_ref, w_conv_ref, w_f_ref, b_f_ref,
                  gain_ref, w_o_ref, ln_g_ref, ln_b_ref,
                  x1_ref, conv_ref, s_ref,
                  h_s, hb_s, u_s, b_s, qk_s, dg_s, qs_s, kd_s, v_s, go_s, st_s, a_s, kv_s, stb_s, mix_s,
                  *, nb, tc, chunk, gate_rank, alpha, side_work=None):
    d_model = x_ref.shape[-1]
    d_conv = conv_ref.shape[-1]
    head_k, head_v = s_ref.shape[-2], s_ref.shape[-1]
    gla_dk = GLA_HEADS * head_k
    d_gla = GLA_HEADS * head_v
    rows = nb * tc
    n_chunks = tc // chunk
    tot_chunks = rows // chunk
    levels = [chunk >> i for i in range(1, chunk.bit_length())]
    contract0 = (((0,), (0,)), ((), ()))
    contract1 = (((1,), (1,)), ((), ()))

    @pl.when(pl.program_id(1) == 0)
    def _():
        conv_ref[...] = conv0_ref[...]
        for i in range(nb):
            for hd in range(GLA_HEADS):
                st_s[i, hd] = s0_ref[i, hd].T

    h = x_ref[...] * (1.0 + mod_ref[:, 1]) + mod_ref[:, 0]
    h_s[...] = h.reshape(rows, d_model).astype(BF16)

    def proj(lo, width):
        return jnp.dot(h_s[...], w_in_ref[:, lo:lo + width], preferred_element_type=F32)

    o_hb, o_hc, o_hin = 0, d_conv, 2 * d_conv
    o_q = 3 * d_conv
    o_k = o_q + gla_dk
    o_v = o_k + gla_dk
    o_go = o_v + d_gla

    flr = jnp.dot(h_s[...], w_fl_ref[...], preferred_element_type=F32)
    t1 = flr.astype(BF16).astype(F32)
    t2 = (flr - t1).astype(BF16).astype(F32)
    t3 = (flr - t1 - t2).astype(BF16).astype(F32)
    terms = (t1, t2, t3)
    grp = lax.broadcasted_iota(jnp.int32, flr.shape, 1) // gate_rank
    lhs = terms[0]
    for g, term in enumerate(GATE_LHS_TERM):
        if term:
            lhs = jnp.where(grp == g, terms[term], lhs)
    z = jnp.dot(lhs.astype(BF16), w_f_ref[...], preferred_element_type=F32) + b_f_ref[...]
    q = proj(o_q, gla_dk) * (head_k ** -0.5)
    k = proj(o_k, gla_dk)
    lf = jax.nn.log_sigmoid(z) * (1.0 / GATE_TAU)
    lf1 = lf.astype(BF16)
    rem = lf - lf1.astype(F32)
    lf2 = rem.astype(BF16)
    lf3 = (rem - lf2.astype(F32)).astype(BF16)
    ri = lax.broadcasted_iota(jnp.int32, (rows, rows), 0)
    ci = lax.broadcasted_iota(jnp.int32, (rows, rows), 1)
    ltri = ((ri // chunk == ci // chunk) & (ci <= ri)).astype(BF16)
    parts = jnp.dot(ltri, jnp.concatenate([lf1, lf2, lf3], axis=1), preferred_element_type=F32)
    v_s[...] = proj(o_v, d_gla).astype(BF16)
    b = parts[:, 0:gla_dk] + parts[:, gla_dk:2 * gla_dk] + parts[:, 2 * gla_dk:3 * gla_dk]

    def block_rows(block, r):
        picked = [b[n * block + r:n * block + r + 1, :] for n in range(rows // block)]
        spread = jnp.concatenate([jnp.broadcast_to(e, (block, gla_dk)) for e in picked], axis=0)
        return picked, spread

    b_end, b_tot = block_rows(chunk, chunk - 1)
    chunk_decay = [jnp.exp(e) for e in b_end]
    qs_s[...] = (q * jnp.exp(b)).astype(BF16)
    kd_s[...] = (k * jnp.exp(b_tot - b)).astype(BF16)
    for hd in range(GLA_HEADS):
        ks = slice(hd * head_k, (hd + 1) * head_k)
        dg_s[hd] = jnp.sum(q[:, ks] * k[:, ks], axis=-1, keepdims=True)

    n_lane_tiles = gla_dk // LANE
    for lt in range(n_lane_tiles):
        b_s[lt] = b[:, lt * LANE:(lt + 1) * LANE]
    row_id = lax.broadcasted_iota(jnp.int32, (rows, gla_dk), 0)
    groups = rows // SUBLANES
    sub_id = lax.broadcasted_iota(jnp.int32, (groups, SUBLANES, LANE), 1)

    def level_ref(s):
        if s >= SUBLANES:
            return block_rows(2 * s, s)[1]
        tiles = []
        for lt in range(n_lane_tiles):
            out = None
            for r0 in range(s, SUBLANES, 2 * s):
                pick = b_s[lt, pl.ds(r0, groups, stride=SUBLANES), :]
                cand = jnp.broadcast_to(pick[:, None, :], (groups, SUBLANES, LANE))
                out = cand if out is None else jnp.where(sub_id >= r0 - s, cand, out)
            tiles.append(out.reshape(rows, LANE))
        return jnp.concatenate(tiles, axis=1)

    for lev, s in enumerate(levels):
        upper = (row_id & (2 * s - 1)) >= s
        e = b - level_ref(s)
        decay = jnp.exp(jnp.minimum(jnp.where(upper, e, -e), 0.0))
        qk_s[lev] = (jnp.where(upper, q, k) * decay).astype(BF16)
    go_s[...] = proj(o_go, d_gla)

    cr = lax.broadcasted_iota(jnp.int32, (chunk, chunk), 0)
    cc = lax.broadcasted_iota(jnp.int32, (chunk, chunk), 1)
    diagonal = cr == cc
    level_masks = [((cr & -(2 * s)) == (cc & -(2 * s))) & ((cr & s) != 0) & ((cc & s) == 0)
                   for s in levels]
    gain = gain_ref[...]

    head_chunks = [(n, hd) for n in range(tot_chunks) for hd in range(GLA_HEADS)]

    def slices(n, hd):
        return (slice(n * chunk, (n + 1) * chunk), slice(hd * head_k, (hd + 1) * head_k),
                slice(hd * head_v, (hd + 1) * head_v))

    def gla_scores_and_updates():
        for n, hd in head_chunks:
            rs, ks, vs = slices(n, hd)
            a = jnp.where(diagonal, dg_s[hd, rs, :], 0.0)
            for lev, mask in enumerate(level_masks):
                f = qk_s[lev, rs, ks]
                p = lax.dot_general(f, f, contract1, preferred_element_type=F32)
                a = jnp.where(mask, p, a)
            a_s[hd, rs, :] = a.astype(BF16)
            kv_s[n, hd] = lax.dot_general(v_s[rs, vs], kd_s[rs, ks], contract0,
                                          preferred_element_type=F32)

    def gla_state_recurrence():
        for n, hd in head_chunks:
            _, ks, _ = slices(n, hd)
            st = st_s[n // n_chunks, hd]
            stb_s[n, hd] = st.T.astype(BF16)
            st_s[n // n_chunks, hd] = st * chunk_decay[n][:, ks] + kv_s[n, hd]

    def gla_outputs():
        for n, hd in head_chunks:
            rs, ks, vs = slices(n, hd)
            lhs = jnp.concatenate([qs_s[rs, ks], a_s[hd, rs, :]], axis=1)
            rhs = jnp.concatenate([stb_s[n, hd], v_s[rs, vs]], axis=0)
            o = jnp.dot(lhs, rhs, preferred_element_type=F32)
            o = o * lax.rsqrt(jnp.mean(o * o, axis=-1, keepdims=True) + RMS_EPS) * gain
            mix_s[rs, d_conv + hd * head_v:d_conv + (hd + 1) * head_v] = (
                o * _silu(go_s[rs, vs])).astype(BF16)

    row = lax.broadcasted_iota(jnp.int32, (tc, d_conv), 0)
    w0, w1, w2 = w_conv_ref[0:1, :], w_conv_ref[1:2, :], w_conv_ref[2:3, :]

    def conv_segment(i):
        rs = slice(i * tc, (i + 1) * tc)
        u = u_s[rs, :]
        st = conv_ref[i]
        p0, p1 = st[0:1, :], st[1:2, :]
        u1 = jnp.where(row == 0, p1, pltpu.roll(u, 1, axis=0))
        u2 = jnp.where(row == 0, p0, jnp.where(row == 1, p1, pltpu.roll(u, 2, axis=0)))
        conv = u2 * w0 + u1 * w1 + u * w2
        mix_s[rs, 0:d_conv] = (hb_s[rs, :] * conv).astype(BF16)
        conv_ref[i] = u[tc - (CONV_WIDTH - 1):tc, :]

    gla_scores_and_updates()
    if side_work is not None:
        side_work()
    u_s[...] = proj(o_hc, d_conv) * proj(o_hin, d_conv)
    gla_state_recurrence()
    hb_s[...] = proj(o_hb, d_conv)
    for i in range(nb):
        conv_segment(i)
    mo = jnp.dot(mix_s[:, 0:d_conv], w_o_ref[0:d_conv, :], preferred_element_type=F32)
    gla_outputs()
    half = rows // 2
    for r in (0, half):
        mo_r = mo[r:r + half] + jnp.dot(mix_s[r:r + half, d_conv:d_model],
                                        w_o_ref[d_conv:d_model, :], preferred_element_type=F32)
        if nb == 1:
            x1_ref[0, r:r + half, :] = _residual_layernorm(
                x_ref[0, r:r + half, :], mod_ref[0, 2], mo_r, ln_g_ref[...], ln_b_ref[...], alpha)
        else:
            bs_ = slice(r // tc, (r + half) // tc)
            x1_ref[bs_] = _residual_layernorm(
                x_ref[bs_], mod_ref[bs_, 2], mo_r.reshape(-1, tc, d_model),
                ln_g_ref[...], ln_b_ref[...], alpha)

    @pl.when(pl.program_id(1) == pl.num_programs(1) - 1)
    def _():
        for i in range(nb):
            for hd in range(GLA_HEADS):
                s_ref[i, hd] = st_s[i, hd].T


def _mixer_cast_kernel(*refs, n_in, n_cast, **static):
    ins, rest = refs[:n_in], refs[n_in:]
    cast_in, rest = rest[:n_cast], rest[n_cast:]
    outs, rest = rest[:3], rest[3:]
    cast_out, scratch = rest[:n_cast], rest[n_cast:]
    def cast_slices():
        for src, dst in zip(cast_in, cast_out):
            dst[...] = src[...].astype(dst.dtype)

    _mixer_kernel(*ins, *outs, *scratch, side_work=cast_slices, **static)


def _cast_block_rows(n_rows, n_steps):
    return next(m for m in range(BF16_ROWS, n_rows + 1, BF16_ROWS)
                if n_rows % m == 0 and m * n_steps >= n_rows)


def _mixer(x, mod, conv0, s0, w_in, w_fl, w_conv, w_f, b_f, gain, w_o, ln_g, ln_b,
           *, nb, tc, chunk, gate_rank, alpha, cast=()):
    bsz, t, d = x.shape
    d_conv = conv0.shape[-1]
    heads, head_k, head_v = s0.shape[1:]
    gla_dk, d_gla = heads * head_k, heads * head_v
    rows = nb * tc
    t_tiles = t // tc
    n_steps = (bsz // nb) * t_tiles
    const2 = lambda bi, ti: (0, 0)
    resident = functools.partial(pl.BlockSpec, index_map=const2, pipeline_mode=pl.Buffered(1))
    static = dict(nb=nb, tc=tc, chunk=chunk, gate_rank=gate_rank, alpha=alpha)
    kern = functools.partial(_mixer_cast_kernel, n_in=13, n_cast=len(cast), **static)

    def cast_spec(w):
        m = _cast_block_rows(w.shape[0], n_steps)
        last = w.shape[0] // m - 1
        return pl.BlockSpec((m, w.shape[1]),
                            lambda bi, ti: (jnp.minimum(bi * t_tiles + ti, last), 0))

    cast_specs = [cast_spec(w) for w in cast]
    return pl.pallas_call(
        kern,
        grid=(bsz // nb, t_tiles),
        in_specs=[
            pl.BlockSpec((nb, tc, d), lambda bi, ti: (bi, ti, 0)),
            pl.BlockSpec((nb, 6, 1, d), lambda bi, ti: (bi, 0, 0, 0)),
            pl.BlockSpec((nb, CONV_WIDTH - 1, d_conv), lambda bi, ti: (bi, 0, 0)),
            pl.BlockSpec((nb, heads, head_k, head_v), lambda bi, ti: (bi, 0, 0, 0)),
            resident(w_in.shape),
            resident(w_fl.shape),
            pl.BlockSpec(w_conv.shape, const2),
            pl.BlockSpec(w_f.shape, const2),
            pl.BlockSpec(b_f.shape, const2),
            pl.BlockSpec(gain.shape, const2),
            resident(w_o.shape),
            pl.BlockSpec(ln_g.shape, const2),
            pl.BlockSpec(ln_b.shape, const2),
        ] + cast_specs,
        out_specs=[
            pl.BlockSpec((nb, tc, d), lambda bi, ti: (bi, ti, 0)),
            pl.BlockSpec((nb, CONV_WIDTH - 1, d_conv), lambda bi, ti: (bi, 0, 0)),
            pl.BlockSpec((nb, heads, head_k, head_v), lambda bi, ti: (bi, 0, 0, 0)),
        ] + cast_specs,
        out_shape=[
            jax.ShapeDtypeStruct((bsz, t, d), F32),
            jax.ShapeDtypeStruct(conv0.shape, F32),
            jax.ShapeDtypeStruct(s0.shape, F32),
        ] + [jax.ShapeDtypeStruct(w.shape, BF16) for w in cast],
        scratch_shapes=[
            pltpu.VMEM((rows, d), BF16),
            pltpu.VMEM((rows, d_conv), F32),
            pltpu.VMEM((rows, d_conv), F32),
            pltpu.VMEM((gla_dk // LANE, rows, LANE), F32),
            pltpu.VMEM((chunk.bit_length() - 1, rows, gla_dk), BF16),
            pltpu.VMEM((heads, rows, 1), F32),
            pltpu.VMEM((rows, gla_dk), BF16),
            pltpu.VMEM((rows, gla_dk), BF16),
            pltpu.VMEM((rows, d_gla), BF16),
            pltpu.VMEM((rows, d_gla), F32),
            pltpu.VMEM((nb, heads, head_v, head_k), F32),
            pltpu.VMEM((heads, rows, chunk), BF16),
            pltpu.VMEM((rows // chunk, heads, head_v, head_k), F32),
            pltpu.VMEM((rows // chunk, heads, head_k, head_v), BF16),
            pltpu.VMEM((rows, d), BF16),
        ],
        compiler_params=pltpu.CompilerParams(dimension_semantics=("arbitrary", "arbitrary"),
                                             vmem_limit_bytes=VMEM_LIMIT),
        name="mixer",
    )(x, mod, conv0, s0, w_in, w_fl, w_conv, w_f, b_f, gain, w_o, ln_g, ln_b, *cast)


def _ffn_kernel(x1_ref, mod_ref, wg_ref, wu_ref, wd_ref, ln_g_ref, ln_b_ref, o_ref,
                h2_s, *, nb, tc, sub, alpha):
    d_model = x1_ref.shape[-1]
    rows = nb * tc
    f = pl.program_id(1)
    last = pl.num_programs(1) - 1

    if nb == 1:
        subs = [(slice(0, 1), slice(q, q + sub), slice(q, q + sub)) for q in range(0, tc, sub)]
    else:
        subs = [(slice(0, nb), slice(0, tc), slice(0, rows))]

    def swiglu(rows_flat):
        h2 = h2_s[rows_flat, :]
        a = jnp.dot(h2, wg_ref[...], preferred_element_type=F32)
        b = jnp.dot(h2, wu_ref[...], preferred_element_type=F32)
        part = jnp.dot((_silu(a) * b).astype(BF16), wd_ref[...], preferred_element_type=F32)
        return part.reshape(-1, tc if nb > 1 else sub, d_model)

    @pl.when(f == 0)
    def _():
        for bs_, rs_, fs_ in subs:
            h2 = x1_ref[bs_, rs_, :] * (1.0 + mod_ref[bs_, 4]) + mod_ref[bs_, 3]
            h2_s[fs_, :] = h2.reshape(-1, d_model).astype(BF16)
            o_ref[bs_, rs_, :] = swiglu(fs_)

    @pl.when((f > 0) & (f < last))
    def _():
        for bs_, rs_, fs_ in subs:
            o_ref[bs_, rs_, :] += swiglu(fs_)

    @pl.when(f == last)
    def _():
        for bs_, rs_, fs_ in subs:
            acc = o_ref[bs_, rs_, :] + swiglu(fs_)
            o_ref[bs_, rs_, :] = _residual_layernorm(
                x1_ref[bs_, rs_, :], mod_ref[bs_, 5], acc, ln_g_ref[...], ln_b_ref[...], alpha)


def _ffn(x1, mod, w_gate, w_up, w_down, ln_g, ln_b, *, nb, tc, tf, sub, alpha):
    bsz, t, d = x1.shape
    d_ff = w_gate.shape[1]
    rows = nb * tc
    t_tiles = t // tc
    assert d_ff % tf == 0 and d_ff // tf >= 2, "the first and last d_ff steps are distinct"
    kern = functools.partial(_ffn_kernel, nb=nb, tc=tc, sub=sub, alpha=alpha)
    x_map = lambda ri, fi: (ri // t_tiles, ri % t_tiles, 0)
    return pl.pallas_call(
        kern,
        grid=((bsz // nb) * t_tiles, d_ff // tf),
        in_specs=[
            pl.BlockSpec((nb, tc, d), x_map),
            pl.BlockSpec((nb, 6, 1, d), lambda ri, fi: (ri // t_tiles, 0, 0, 0)),
            pl.BlockSpec((d, tf), lambda ri, fi: (0, fi)),
            pl.BlockSpec((d, tf), lambda ri, fi: (0, fi)),
            pl.BlockSpec((tf, d), lambda ri, fi: (fi, 0)),
            pl.BlockSpec(ln_g.shape, lambda ri, fi: (0, 0)),
            pl.BlockSpec(ln_b.shape, lambda ri, fi: (0, 0)),
        ],
        out_specs=pl.BlockSpec((nb, tc, d), x_map),
        out_shape=jax.ShapeDtypeStruct((bsz, t, d), F32),
        scratch_shapes=[pltpu.VMEM((rows, d), BF16)],
        compiler_params=pltpu.CompilerParams(dimension_semantics=("arbitrary", "arbitrary"),
                                             vmem_limit_bytes=VMEM_LIMIT),
        name="ffn",
    )(x1, mod, w_gate, w_up, w_down, ln_g, ln_b)


def kernel(x_prompt, x_sample, c_prompt, c_sample, cache_conv, state_gla, w_ada, b_ada, w_in,
           w_conv, w_f, b_f, gla_gain, w_o, ln1_g, ln1_b, w_gate, w_up, w_down, ln2_g, ln2_b):
    depth = w_ada.shape[0]
    bp, _, d_model = x_prompt.shape
    bs, t_s, _ = x_sample.shape
    d_conv = cache_conv.shape[-1]
    heads, head_k, head_v = state_gla.shape[2:]
    gate_rank = w_f.shape[1]
    alpha = (2.0 * depth) ** 0.25

    yp, ys = x_prompt, x_sample
    conv_p, gla_p, conv_s, gla_s = [], [], [], []
    for l in range(depth):
        c_all = jnp.concatenate([c_prompt, c_sample], axis=0)
        pad_rows = -c_all.shape[0] % 8
        c_all = jnp.pad(c_all, ((0, pad_rows), (0, 0)))
        mod = _modulation(c_all, w_ada[l], b_ada[l], tn=MOD_TN)
        mod_p = mod[:bp].reshape(bp, 6, 1, d_model)
        mod_s = mod[bp:bp + bs].reshape(bs, 6, 1, d_model)

        n_main = w_in.shape[-1] - gate_rank
        copies = len(GATE_LHS_TERM)
        fill = LANE - copies * gate_rank
        w_in_b = w_in[l].astype(BF16)
        w_fl_b = jnp.concatenate([w_in[l][:, n_main:]] * copies
                                 + [jnp.zeros((d_model, fill), F32)], axis=1).astype(BF16)
        wf_terms = _bf16_terms(w_f[l])
        w_f_p = jnp.concatenate([wf_terms[t] for t in GATE_RHS_TERM]
                                + [jnp.zeros((fill, w_f.shape[-1]), BF16)], axis=0)
        w_o_b = w_o[l].astype(BF16)
        row = lambda a: a.reshape(1, -1)
        mixer_w = (w_in_b, w_fl_b, w_conv[l], w_f_p, row(b_f[l]), row(gla_gain[l]), w_o_b,
                   row(ln1_g[l]), row(ln1_b[l]))

        conv0 = jnp.zeros((bp, CONV_WIDTH - 1, d_conv), F32)
        gla0 = jnp.zeros((bp, heads, head_k, head_v), F32)
        x1p, cp, sp, wg_b, wu_b, wd_b = _mixer(
            yp, mod_p, conv0, gla0, *mixer_w, gate_rank=gate_rank, alpha=alpha,
            cast=(w_gate[l], w_up[l], w_down[l]), **PROMPT_MIXER)
        ffn_w = (wg_b, wu_b, wd_b, row(ln2_g[l]), row(ln2_b[l]))
        yp = _ffn(x1p, mod_p, *ffn_w, alpha=alpha, **PROMPT_FFN)

        x1s, cs, ss = _mixer(ys, mod_s, cache_conv[l], state_gla[l], *mixer_w,
                             nb=SAMPLE_MIXER_NB, tc=t_s, chunk=t_s, gate_rank=gate_rank,
                             alpha=alpha)
        ys = _ffn(x1s, mod_s, *ffn_w, nb=bs, tc=t_s, tf=SAMPLE_FFN_TF, sub=t_s, alpha=alpha)

        conv_p.append(cp)
        gla_p.append(sp)
        conv_s.append(cs)
        gla_s.append(ss)
    return (yp, ys, jnp.stack(conv_p), jnp.stack(gla_p), jnp.stack(conv_s), jnp.stack(gla_s))
```
